```python
import jax, jax.numpy as jnp
from jax import lax
import numpy as np

D_MODEL = 4096
BATCH = 1
SEQ = 8192
DEPTH = 1
DEC_BATCH = 128
DEC_SEQ = 4
PAST_LEN = 8192
PAGE_SIZE = 128

HEAD_DIM = 64
N_HEADS = D_MODEL // 128
N_KV_HEADS = N_HEADS // 8
N_GROUP = N_HEADS // N_KV_HEADS
WINDOW = 128
BLOCK = 128
ROPE_THETA = 10000.0
CONV_DIM = D_MODEL // 2
CONV_WIDTH = 31
D_FF = ((8 * D_MODEL // 3 + 255) // 256) * 256
Q_DIM = N_HEADS * HEAD_DIM
KV_DIM = N_KV_HEADS * HEAD_DIM
IN_DIM = Q_DIM + 2 * KV_DIM + 2 * CONV_DIM + 2 * D_MODEL
SPLITS = (Q_DIM, Q_DIM + KV_DIM, Q_DIM + 2 * KV_DIM, Q_DIM + 2 * KV_DIM + 2 * CONV_DIM)
EPS = 1e-6
NEG = -1e30

kernel_name = "hybrid_swa_sink_conformer_conv_step"


def _rmsnorm(x, g):
    x32 = x.astype(jnp.float32)
    y = x32 * lax.rsqrt(jnp.mean(x32 * x32, axis=-1, keepdims=True) + EPS) * g.astype(jnp.float32)
    return y.astype(x.dtype)


def _layernorm(x, g, b):
    x32 = x.astype(jnp.float32)
    mu = jnp.mean(x32, axis=-1, keepdims=True)
    var = jnp.mean(jnp.square(x32 - mu), axis=-1, keepdims=True)
    y = (x32 - mu) * lax.rsqrt(var + EPS) * g.astype(jnp.float32) + b.astype(jnp.float32)
    return y.astype(x.dtype)


def _rope(x, pos):
    half = HEAD_DIM // 2
    inv_freq = ROPE_THETA ** (-jnp.arange(half, dtype=jnp.float32) / half)
    ang = pos.astype(jnp.float32)[:, None] * inv_freq[None, :]
    cos = jnp.cos(ang)[None, :, None, :]
    sin = jnp.sin(ang)[None, :, None, :]
    x32 = x.astype(jnp.float32)
    x1, x2 = x32[..., :half], x32[..., half:]
    return jnp.concatenate([x1 * cos - x2 * sin, x2 * cos + x1 * sin], axis=-1).astype(x.dtype)


def _sink_attend(s, mask, sinks, v, eq):
    s = jnp.where(mask, s, NEG)
    sk = sinks.astype(jnp.float32).reshape(N_KV_HEADS, N_GROUP)[..., None]
    m = jnp.maximum(jnp.max(s, axis=-1), sk)
    p = jnp.exp(s - m[..., None])
    p = p / (jnp.sum(p, axis=-1) + jnp.exp(sk - m))[..., None]
    return jnp.einsum(eq, p.astype(v.dtype), v)


def _attn_prompt(q, k, v, sinks):
    B, T = q.shape[0], q.shape[1]
    nb = T // BLOCK
    qb = q.reshape(B, nb, BLOCK, N_KV_HEADS, N_GROUP, HEAD_DIM)
    kb = k.reshape(B, nb, BLOCK, N_KV_HEADS, HEAD_DIM)
    vb = v.reshape(B, nb, BLOCK, N_KV_HEADS, HEAD_DIM)
    pad = ((0, 0), (1, 0), (0, 0), (0, 0), (0, 0))
    kk = jnp.concatenate([jnp.pad(kb, pad)[:, :nb], kb], axis=2)
    vv = jnp.concatenate([jnp.pad(vb, pad)[:, :nb], vb], axis=2)
    s = jnp.einsum('bnqkgd,bnskd->bnkgqs', qb, kk,
                   preferred_element_type=jnp.float32) * (HEAD_DIM ** -0.5)
    n = jnp.arange(nb)[:, None, None]
    qi = jnp.arange(BLOCK)[None, :, None] + BLOCK
    kj = jnp.arange(2 * BLOCK)[None, None, :]
    diff = qi - kj
    mask = (diff >= 0) & (diff < WINDOW) & ((n > 0) | (kj >= BLOCK))
    o = _sink_attend(s, mask[:, None, None], sinks, vv, 'bnkgqs,bnskd->bnqkgd')
    return o.reshape(B, T, Q_DIM)


def _attn_sample(q, k_all, v_all, sinks, qpos, kpos):
    B, T = q.shape[0], q.shape[1]
    qg = q.reshape(B, T, N_KV_HEADS, N_GROUP, HEAD_DIM)
    s = jnp.einsum('btkgd,bskd->bkgts', qg, k_all,
                   preferred_element_type=jnp.float32) * (HEAD_DIM ** -0.5)
    diff = qpos[:, None] - kpos[None, :]
    mask = (diff >= 0) & (diff < WINDOW)
    o = _sink_attend(s, mask, sinks, v_all, 'bkgts,bskd->btkgd')
    return o.reshape(B, T, Q_DIM)


def _dwconv(xs, w, b):
    y = lax.conv_general_dilated(xs, w[:, None, :].astype(xs.dtype), window_strides=(1,),
                                 padding='VALID', dimension_numbers=('NWC', 'WIO', 'NWC'),
                                 feature_group_count=xs.shape[-1])
    return y + b.astype(xs.dtype)


def _forward(x, pos, cache_k, cache_v, state_conv, g_mix_norm, w_in, sinks, w_attn_o, w_dw, b_dw,
             g_conv_ln, b_conv_ln, w_conv_o, w_out, g_ffn_norm, w_ffn_in, w_ffn_out, g_final):
    B, T, _ = x.shape
    ks, vs, cs = [], [], []
    for l in range(DEPTH):
        h = _rmsnorm(x, g_mix_norm[l])
        z = h @ w_in[l]
        q, k, v, u, gate = jnp.split(z, SPLITS, axis=-1)
        q = _rope(q.reshape(B, T, N_HEADS, HEAD_DIM), pos)
        k = _rope(k.reshape(B, T, N_KV_HEADS, HEAD_DIM), pos)
        v = v.reshape(B, T, N_KV_HEADS, HEAD_DIM)
        if cache_k is None:
            o = _attn_prompt(q, k, v, sinks[l])
            k_all, v_all = k, v
            conv_past = jnp.zeros((B, CONV_WIDTH - 1, CONV_DIM), x.dtype)
        else:
            k_all = jnp.concatenate([cache_k[l].astype(k.dtype), k], axis=1)
            v_all = jnp.concatenate([cache_v[l].astype(v.dtype), v], axis=1)
            n_past = cache_k.shape[2]
            kpos = pos[0] - n_past + jnp.arange(n_past + T)
            o = _attn_sample(q, k_all, v_all, sinks[l], pos, kpos)
            conv_past = state_conv[l].astype(x.dtype)
        ks.append(k_all[:, k_all.shape[1] - WINDOW:])
        vs.append(v_all[:, v_all.shape[1] - WINDOW:])
        y_a = o @ w_attn_o[l]
        ua, ub = jnp.split(u, 2, axis=-1)
        cu = ua * jax.nn.sigmoid(ub)
        xs = jnp.concatenate([conv_past, cu], axis=1)
        cs.append(xs[:, xs.shape[1] - (CONV_WIDTH - 1):])
        c = _dwconv(xs, w_dw[l], b_dw[l])
        c = jax.nn.silu(_layernorm(c, g_conv_ln[l], b_conv_ln[l]))
        y_c = c @ w_conv_o[l]
        g_a, g_c = jnp.split(jax.nn.sigmoid(gate), 2, axis=-1)
        x = x + (g_a * y_a + g_c * y_c) @ w_out[l]
        h = _rmsnorm(x, g_ffn_norm[l])
        fg, fu = jnp.split(h @ w_ffn_in[l], 2, axis=-1)
        x = x + (jax.nn.silu(fg) * fu) @ w_ffn_out[l]
    return _rmsnorm(x, g_final), jnp.stack(ks), jnp.stack(vs), jnp.stack(cs)


def setup_inputs(seed: int = 0) -> dict:
    key = jax.random.key(seed)
    ks = jax.random.split(key, 24)
    f32 = jnp.float32
    n_win = min(WINDOW, PAST_LEN)
    nrm = lambda k, shape, scale: jax.random.normal(k, shape, f32) * scale
    return {
        "x_prompt": nrm(ks[0], (BATCH, SEQ, D_MODEL), 1.0),
        "x_sample": nrm(ks[1], (DEC_BATCH, DEC_SEQ, D_MODEL), 1.0),
        "cache_k": nrm(ks[2], (DEPTH, DEC_BATCH, n_win, N_KV_HEADS, HEAD_DIM), 1.0),
        "cache_v": nrm(ks[3], (DEPTH, DEC_BATCH, n_win, N_KV_HEADS, HEAD_DIM), 1.0),
        "state_conv": nrm(ks[4], (DEPTH, DEC_BATCH, CONV_WIDTH - 1, CONV_DIM), 0.5),
        "g_mix_norm": 1.0 + nrm(ks[5], (DEPTH, D_MODEL), 0.01),
        "w_in": nrm(ks[6], (DEPTH, D_MODEL, IN_DIM), D_MODEL ** -0.5),
        "sinks": nrm(ks[7], (DEPTH, N_HEADS), 1.0),
        "w_attn_o": nrm(ks[8], (DEPTH, Q_DIM, D_MODEL), Q_DIM ** -0.5),
        "w_dw": nrm(ks[9], (DEPTH, CONV_WIDTH, CONV_DIM), CONV_WIDTH ** -0.5),
        "b_dw": nrm(ks[10], (DEPTH, CONV_DIM), 0.01),
        "g_conv_ln": 1.0 + nrm(ks[11], (DEPTH, CONV_DIM), 0.01),
        "b_conv_ln": nrm(ks[12], (DEPTH, CONV_DIM), 0.01),
        "w_conv_o": nrm(ks[13], (DEPTH, CONV_DIM, D_MODEL), CONV_DIM ** -0.5),
        "w_out": nrm(ks[14], (DEPTH, D_MODEL, D_MODEL), D_MODEL ** -0.5),
        "g_ffn_norm": 1.0 + nrm(ks[15], (DEPTH, D_MODEL), 0.01),
        "w_ffn_in": nrm(ks[16], (DEPTH, D_MODEL, 2 * D_FF), D_MODEL ** -0.5),
        "w_ffn_out": nrm(ks[17], (DEPTH, D_FF, D_MODEL), D_FF ** -0.5),
        "g_final": 1.0 + nrm(ks[18], (D_MODEL,), 0.01),
    }


def reference(x_prompt, x_sample, cache_k, cache_v, state_conv, g_mix_norm, w_in, sinks, w_attn_o,
              w_dw, b_dw, g_conv_ln, b_conv_ln, w_conv_o, w_out, g_ffn_norm, w_ffn_in, w_ffn_out,
              g_final):
    pos_prompt = jnp.arange(x_prompt.shape[1], dtype=jnp.int32)
    pos_sample = PAST_LEN + jnp.arange(x_sample.shape[1], dtype=jnp.int32)
    y_prompt, k_prompt, v_prompt, conv_prompt = _forward(
        x_prompt, pos_prompt, None, None, None, g_mix_norm, w_in, sinks, w_attn_o, w_dw, b_dw,
        g_conv_ln, b_conv_ln, w_conv_o, w_out, g_ffn_norm, w_ffn_in, w_ffn_out, g_final)
    y_sample, k_sample, v_sample, conv_sample = _forward(
        x_sample, pos_sample, cache_k, cache_v, state_conv, g_mix_norm, w_in, sinks, w_attn_o, w_dw,
        b_dw, g_conv_ln, b_conv_ln, w_conv_o, w_out, g_ffn_norm, w_ffn_in, w_ffn_out, g_final)
    return (y_prompt, y_sample, k_prompt, v_prompt, conv_prompt, k_sample, v_sample, conv_sample)
```

```python
import functools

import jax
import jax.numpy as jnp
from jax import lax
from jax.experimental import pallas as pl
from jax.experimental.pallas import tpu as pltpu

D_MODEL = 4096
HEAD_DIM = 64
N_HEADS = 32
N_KV_HEADS = 4
N_GROUP = N_HEADS // N_KV_HEADS
WINDOW = 128
ROPE_THETA = 10000.0
CONV_DIM = 2048
CONV_WIDTH = 31
D_FF = 11008
PAST_LEN = 8192
Q_DIM = N_HEADS * HEAD_DIM
KV_DIM = N_KV_HEADS * HEAD_DIM
EPS = 1e-6
NEG = -1e30

V7X_LANES = 128
V7X_VMEM_LIMIT_BYTES = 60000 * 1024

F32 = jnp.float32
BF16 = jnp.bfloat16

TOKEN_TILE = 512
WEIGHT_BLOCK = 512
CAST_ROWS = 256


def _params(*sem):
    return pltpu.CompilerParams(dimension_semantics=sem,
                                vmem_limit_bytes=V7X_VMEM_LIMIT_BYTES)


def _rmsnorm_kernel(x_ref, g_ref, o_ref):
    x = x_ref[...]
    y = x * lax.rsqrt(jnp.mean(x * x, axis=-1, keepdims=True) + EPS) * g_ref[...]
    o_ref[...] = y.astype(o_ref.dtype)


def _rmsnorm(x, g, out_dtype, name):
    m, d = x.shape
    tm = 256
    return pl.pallas_call(
        _rmsnorm_kernel,
        grid=(m // tm,),
        in_specs=[pl.BlockSpec((tm, d), lambda i: (i, 0)),
                  pl.BlockSpec((1, d), lambda i: (0, 0))],
        out_specs=pl.BlockSpec((tm, d), lambda i: (i, 0)),
        out_shape=jax.ShapeDtypeStruct((m, d), out_dtype),
        compiler_params=_params("arbitrary"),
        name=name,
    )(x, g.reshape(1, d))


def _cast_weight(w_ref, wbf_ref, col0):
    k, bw = w_ref.shape

    def body(c, carry):
        r = pl.multiple_of(c * CAST_ROWS, CAST_ROWS)
        wbf_ref[pl.ds(r, CAST_ROWS), col0:col0 + bw] = w_ref[pl.ds(r, CAST_ROWS), :].astype(BF16)
        return carry

    lax.fori_loop(0, k // CAST_ROWS, body, 0)


def _wres_kernel(*refs, n_terms, n_wblk, n_extra, n_out, epilogue):
    a_refs = refs[:n_terms]
    pos = n_terms
    w_refs = []
    for t in range(n_terms):
        w_refs.append(refs[pos:pos + n_wblk[t]])
        pos += n_wblk[t]
    extra_refs = refs[pos:pos + n_extra]
    pos += n_extra
    out_refs = refs[pos:pos + n_out]
    pos += n_out
    wbf_refs = refs[pos:pos + n_terms]

    @pl.when(pl.program_id(1) == 0)
    def _():
        for t in range(n_terms):
            bw = w_refs[t][0].shape[1]
            for b, w_ref in enumerate(w_refs[t]):
                _cast_weight(w_ref, wbf_refs[t], b * bw)

    accs = [jnp.dot(a_refs[t][...], wbf_refs[t][...], preferred_element_type=F32)
            for t in range(n_terms)]
    outs = epilogue(accs, [r[...] for r in extra_refs])
    for o_ref, o in zip(out_refs, outs):
        o_ref[...] = o.astype(o_ref.dtype)


def _wres_matmul(terms, extras, outs, epilogue, *, n_col_steps, name, tm=TOKEN_TILE,
                 weight_buffers=2):
    m = terms[0][0].shape[0]
    in_arrays, in_specs, scratch, n_wblk = [], [], [], []
    for a, w, bw, col_fns in terms:
        in_arrays.append(a)
        in_specs.append(pl.BlockSpec((tm, a.shape[1]), lambda j, i: (i, 0)))
    for a, w, bw, col_fns in terms:
        k = w.shape[0]
        for fn in col_fns:
            in_arrays.append(w)
            in_specs.append(pl.BlockSpec((k, bw), functools.partial(lambda j, i, fn: (0, fn(j)), fn=fn),
                                         pipeline_mode=pl.Buffered(weight_buffers)))
        n_wblk.append(len(col_fns))
        scratch.append(pltpu.VMEM((k, bw * len(col_fns)), BF16))
    for arr, blk, imap in extras:
        in_arrays.append(arr)
        in_specs.append(pl.BlockSpec(blk, imap))
    kernel = functools.partial(_wres_kernel, n_terms=len(terms), n_wblk=tuple(n_wblk),
                               n_extra=len(extras), n_out=len(outs), epilogue=epilogue)
    res = pl.pallas_call(
        kernel,
        grid=(n_col_steps, m // tm),
        in_specs=in_specs,
        out_specs=[pl.BlockSpec(blk, imap) for _, blk, imap in outs],
        out_shape=[s for s, _, _ in outs],
        scratch_shapes=scratch,
        compiler_params=_params("arbitrary", "arbitrary"),
        name=name,
    )(*in_arrays)
    return res


def _ares_kernel(a_ref, w_ref, x_ref, o_ref):
    acc = jnp.dot(a_ref[...], w_ref[...], preferred_element_type=F32)
    o_ref[...] = x_ref[...] + acc


def _ares_matmul_residual(a, w_bf, x, *, tm, tn, name):
    m, k = a.shape
    n = w_bf.shape[1]
    return pl.pallas_call(
        _ares_kernel,
        grid=(m // tm, n // tn),
        in_specs=[pl.BlockSpec((tm, k), lambda i, j: (i, 0)),
                  pl.BlockSpec((k, tn), lambda i, j: (0, j)),
                  pl.BlockSpec((tm, tn), lambda i, j: (i, j))],
        out_specs=pl.BlockSpec((tm, tn), lambda i, j: (i, j)),
        out_shape=jax.ShapeDtypeStruct((m, n), F32),
        compiler_params=_params("arbitrary", "arbitrary"),
        name=name,
    )(a, w_bf, x)


def _rope(x, cos128, sin128):
    n = x.shape[1]
    reps = n // V7X_LANES
    c = jnp.concatenate([cos128] * reps, axis=1) if reps > 1 else cos128
    s = jnp.concatenate([sin128] * reps, axis=1) if reps > 1 else sin128
    lane = lax.broadcasted_iota(jnp.int32, x.shape, 1)
    first_half = (lane & (HEAD_DIM - 1)) < (HEAD_DIM // 2)
    partner = jnp.where(first_half,
                        pltpu.roll(x, n - HEAD_DIM // 2, axis=1),
                        pltpu.roll(x, HEAD_DIM // 2, axis=1))
    return x * c + partner * s


def _epi_q(accs, extras):
    cos, sin = extras
    return [_rope(accs[0], cos, sin)]


def _epi_kv(accs, extras):
    cos, sin = extras
    acc = accs[0]
    return [_rope(acc[:, :KV_DIM], cos, sin), acc[:, KV_DIM:]]


def _epi_glu(accs, extras):
    acc = accs[0]
    half = acc.shape[1] // 2
    return [acc[:, :half] * jax.nn.sigmoid(acc[:, half:])]


def _epi_sigmoid(accs, extras):
    return [jax.nn.sigmoid(accs[0])]


def _epi_merge(accs, extras):
    g_a, g_c = extras
    return [g_a.astype(F32) * accs[0] + g_c.astype(F32) * accs[1]]


def _epi_residual(accs, extras):
    return [extras[0] + accs[0]]


def _epi_swiglu(accs, extras):
    acc = accs[0]
    half = acc.shape[1] // 2
    return [jax.nn.silu(acc[:, :half]) * acc[:, half:]]


def _attn_prompt_kernel(sink_ref, q_ref, kc_ref, kp_ref, vc_ref, vp_ref, o_ref):
    n = pl.program_id(0)
    blk = q_ref.shape[0]
    row = lax.broadcasted_iota(jnp.int32, (N_GROUP * blk, 2 * blk), 0) & (blk - 1)
    col = lax.broadcasted_iota(jnp.int32, (N_GROUP * blk, 2 * blk), 1)
    diff = row + blk - col
    valid = (diff >= 0) & (diff < WINDOW) & ((n > 0) | (col >= blk))
    for kh in range(N_KV_HEADS):
        ls = slice(kh * HEAD_DIM, (kh + 1) * HEAD_DIM)
        kk = jnp.concatenate([kp_ref[:, ls], kc_ref[:, ls]], axis=0).astype(BF16)
        vv = jnp.concatenate([vp_ref[:, ls], vc_ref[:, ls]], axis=0).astype(BF16)
        qg = jnp.concatenate(
            [q_ref[:, (kh * N_GROUP + g) * HEAD_DIM:(kh * N_GROUP + g + 1) * HEAD_DIM]
             for g in range(N_GROUP)], axis=0)
        s = lax.dot_general(qg, kk, (((1,), (1,)), ((), ())),
                            preferred_element_type=F32) * (HEAD_DIM ** -0.5)
        s = jnp.where(valid, s, NEG)
        sk = sink_ref[kh][:, :1]
        m = jnp.maximum(jnp.max(s, axis=-1, keepdims=True), sk)
        p = jnp.exp(s - m)
        p = p / (jnp.sum(p, axis=-1, keepdims=True) + jnp.exp(sk - m))
        o = jnp.dot(p.astype(BF16), vv, preferred_element_type=F32)
        o_ref[:, kh * N_GROUP * HEAD_DIM:(kh + 1) * N_GROUP * HEAD_DIM] = jnp.concatenate(
            [o[g * blk:(g + 1) * blk] for g in range(N_GROUP)], axis=1).astype(o_ref.dtype)


def _attn_prompt(q, k, v, sinks, t_prompt):
    blk = WINDOW
    nb = t_prompt // blk
    sink_rows = jnp.broadcast_to(
        jnp.repeat(sinks.astype(F32).reshape(N_KV_HEADS, N_GROUP), blk, axis=1)[:, :, None],
        (N_KV_HEADS, N_GROUP * blk, V7X_LANES))
    cur = lambda n: (n, 0)
    prev = lambda n: (jnp.maximum(n - 1, 0), 0)
    return pl.pallas_call(
        _attn_prompt_kernel,
        grid=(nb,),
        in_specs=[pl.BlockSpec((N_KV_HEADS, N_GROUP * blk, V7X_LANES), lambda n: (0, 0, 0)),
                  pl.BlockSpec((blk, Q_DIM), cur),
                  pl.BlockSpec((blk, KV_DIM), cur),
                  pl.BlockSpec((blk, KV_DIM), prev),
                  pl.BlockSpec((blk, KV_DIM), cur),
                  pl.BlockSpec((blk, KV_DIM), prev)],
        out_specs=pl.BlockSpec((blk, Q_DIM), cur),
        out_shape=jax.ShapeDtypeStruct((t_prompt, Q_DIM), BF16),
        compiler_params=_params("arbitrary"),
        name="attn_prompt",
    )(sink_rows, q, k, k, v, v)


def _attn_sample_kernel(sink_ref, q_ref, ck_ref, cv_ref, kn_ref, vn_ref, o_ref, ko_ref, vo_ref):
    bb, t_new = kn_ref.shape[0], kn_ref.shape[1]
    n_past = ck_ref.shape[1]
    rows = N_GROUP * t_new
    t_of_row = lax.broadcasted_iota(jnp.int32, (rows, n_past), 0) & (t_new - 1)
    d_past = t_of_row + n_past - lax.broadcasted_iota(jnp.int32, (rows, n_past), 1)
    valid_past = (d_past >= 0) & (d_past < WINDOW)
    t_col = lax.broadcasted_iota(jnp.int32, (rows, 1), 0) & (t_new - 1)
    scale = HEAD_DIM ** -0.5
    dn = (((1,), (1,)), ((), ()))
    for b in range(bb):
        ko_ref[b, 0:n_past - t_new, :] = ck_ref[b, t_new:n_past, :]
        ko_ref[b, n_past - t_new:n_past, :] = kn_ref[b]
        vo_ref[b, 0:n_past - t_new, :] = cv_ref[b, t_new:n_past, :]
        vo_ref[b, n_past - t_new:n_past, :] = vn_ref[b]
        for kh in range(N_KV_HEADS):
            ls = slice(kh * HEAD_DIM, (kh + 1) * HEAD_DIM)
            qg = q_ref[b, kh]
            qf = qg.astype(F32)
            kc = ck_ref[b, :, ls].astype(BF16)
            vc = cv_ref[b, :, ls].astype(BF16)
            kn = kn_ref[b, :, ls].astype(BF16).astype(F32)
            vn = vn_ref[b, :, ls].astype(BF16).astype(F32)
            s_c = jnp.where(valid_past, lax.dot_general(qg, kc, dn, preferred_element_type=F32) * scale, NEG)
            s_n = [jnp.where((t_col >= u) & (t_col - u < WINDOW),
                             jnp.sum(qf * kn[u:u + 1, :], axis=-1, keepdims=True) * scale, NEG)
                   for u in range(t_new)]
            sk = sink_ref[kh][:, :1]
            m = jnp.maximum(jnp.max(s_c, axis=-1, keepdims=True), sk)
            for u in range(t_new):
                m = jnp.maximum(m, s_n[u])
            p_c = jnp.exp(s_c - m)
            p_n = [jnp.exp(s_n[u] - m) for u in range(t_new)]
            denom = jnp.sum(p_c, axis=-1, keepdims=True) + jnp.exp(sk - m)
            for u in range(t_new):
                denom = denom + p_n[u]
            o = jnp.dot((p_c / denom).astype(BF16), vc, preferred_element_type=F32)
            for u in range(t_new):
                o = o + (p_n[u] / denom) * vn[u:u + 1, :]
            o_ref[b, kh] = o.astype(o_ref.dtype)


def _attn_sample(q, k_new, v_new, cache_k, cache_v, sinks):
    b, t_new, _ = k_new.shape
    n_past = cache_k.shape[1]
    assert n_past == WINDOW and t_new & (t_new - 1) == 0
    rows = N_GROUP * t_new
    bb = 8
    sink_rows = jnp.broadcast_to(
        jnp.repeat(sinks.astype(F32).reshape(N_KV_HEADS, N_GROUP), t_new, axis=1)[:, :, None],
        (N_KV_HEADS, rows, V7X_LANES))
    i3 = lambda i: (i, 0, 0)
    i4 = lambda i: (i, 0, 0, 0)
    return pl.pallas_call(
        _attn_sample_kernel,
        grid=(b // bb,),
        in_specs=[pl.BlockSpec((N_KV_HEADS, rows, V7X_LANES), lambda i: (0, 0, 0)),
                  pl.BlockSpec((bb, N_KV_HEADS, rows, HEAD_DIM), i4),
                  pl.BlockSpec((bb, n_past, KV_DIM), i3),
                  pl.BlockSpec((bb, n_past, KV_DIM), i3),
                  pl.BlockSpec((bb, t_new, KV_DIM), i3),
                  pl.BlockSpec((bb, t_new, KV_DIM), i3)],
        out_specs=[pl.BlockSpec((bb, N_KV_HEADS, rows, HEAD_DIM), i4),
                   pl.BlockSpec((bb, n_past, KV_DIM), i3),
                   pl.BlockSpec((bb, n_past, KV_DIM), i3)],
        out_shape=[jax.ShapeDtypeStruct((b, N_KV_HEADS, rows, HEAD_DIM), BF16),
                   jax.ShapeDtypeStruct((b, n_past, KV_DIM), F32),
                   jax.ShapeDtypeStruct((b, n_past, KV_DIM), F32)],
        compiler_params=_params("arbitrary"),
        name="attn_sample",
    )(sink_rows, q, cache_k, cache_v, k_new, v_new)


def _ln_swish(c, g, b):
    mu = jnp.mean(c, axis=-1, keepdims=True)
    var = jnp.mean(jnp.square(c - mu), axis=-1, keepdims=True)
    y = (c - mu) * lax.rsqrt(var + EPS) * g + b
    return jax.nn.silu(y)


CONV_HALO = 32


def _conv_prompt_kernel(cur_ref, halo_ref, w_ref, b_ref, g_ref, bl_ref, o_ref, xs_ref, c_ref):
    i = pl.program_id(0)
    tt = cur_ref.shape[0]
    xs_ref[0:CONV_HALO, :] = jnp.where(i > 0, halo_ref[...], 0.0)
    xs_ref[CONV_HALO:CONV_HALO + tt, :] = cur_ref[...]
    off = CONV_HALO - (CONV_WIDTH - 1)
    for cc in range(cur_ref.shape[1] // V7X_LANES):
        ls = slice(cc * V7X_LANES, (cc + 1) * V7X_LANES)
        acc = xs_ref[off:off + tt, ls] * w_ref[0:1, ls]
        for w in range(1, CONV_WIDTH):
            acc = acc + xs_ref[off + w:off + w + tt, ls] * w_ref[w:w + 1, ls]
        c_ref[:, ls] = acc + b_ref[:, ls]
    o_ref[...] = _ln_swish(c_ref[...], g_ref[...], bl_ref[...]).astype(o_ref.dtype)


def _conv_prompt(cu, w_dw, b_dw, g_ln, b_ln, t_prompt):
    tt = 256
    c = cu.shape[1]
    per = tt // CONV_HALO
    row1 = lambda i: (0, 0)
    return pl.pallas_call(
        _conv_prompt_kernel,
        grid=(t_prompt // tt,),
        in_specs=[pl.BlockSpec((tt, c), lambda i: (i, 0)),
                  pl.BlockSpec((CONV_HALO, c), lambda i: (jnp.maximum(i * per - 1, 0), 0)),
                  pl.BlockSpec((CONV_WIDTH, c), row1),
                  pl.BlockSpec((1, c), row1),
                  pl.BlockSpec((1, c), row1),
                  pl.BlockSpec((1, c), row1)],
        out_specs=pl.BlockSpec((tt, c), lambda i: (i, 0)),
        out_shape=jax.ShapeDtypeStruct((t_prompt, c), BF16),
        scratch_shapes=[pltpu.VMEM((CONV_HALO + tt, c), F32), pltpu.VMEM((tt, c), F32)],
        compiler_params=_params("arbitrary"),
        name="conv_prompt",
    )(cu, cu, w_dw, b_dw.reshape(1, c), g_ln.reshape(1, c), b_ln.reshape(1, c))


def _conv_sample_kernel(st_ref, cu_ref, w_ref, b_ref, g_ref, bl_ref, o_ref, so_ref, xs_ref):
    bb, n_hist, c = st_ref.shape
    t_new = cu_ref.shape[1]
    for b in range(bb):
        xs_ref[0:n_hist, :] = st_ref[b]
        xs_ref[n_hist:n_hist + t_new, :] = cu_ref[b]
        so_ref[b] = xs_ref[t_new:t_new + n_hist, :]
        acc = xs_ref[0:t_new, :] * w_ref[0:1, :]
        for w in range(1, CONV_WIDTH):
            acc = acc + xs_ref[w:w + t_new, :] * w_ref[w:w + 1, :]
        o_ref[b] = _ln_swish(acc + b_ref[...], g_ref[...], bl_ref[...]).astype(o_ref.dtype)


def _conv_sample(cu_new, state, w_dw, b_dw, g_ln, b_ln):
    b, t_new, c = cu_new.shape
    n_hist = state.shape[1]
    bb = 8
    i3 = lambda i: (i, 0, 0)
    row1 = lambda i: (0, 0)
    return pl.pallas_call(
        _conv_sample_kernel,
        grid=(b // bb,),
        in_specs=[pl.BlockSpec((bb, n_hist, c), i3),
                  pl.BlockSpec((bb, t_new, c), i3),
                  pl.BlockSpec((CONV_WIDTH, c), row1),
                  pl.BlockSpec((1, c), row1),
                  pl.BlockSpec((1, c), row1),
                  pl.BlockSpec((1, c), row1)],
        out_specs=[pl.BlockSpec((bb, t_new, c), i3),
                   pl.BlockSpec((bb, n_hist, c), i3)],
        out_shape=[jax.ShapeDtypeStruct((b, t_new, c), BF16),
                   jax.ShapeDtypeStruct((b, n_hist, c), F32)],
        scratch_shapes=[pltpu.VMEM((n_hist + t_new + 6, c), F32)],
        compiler_params=_params("arbitrary"),
        name="conv_sample",
    )(state, cu_new, w_dw, b_dw.reshape(1, c), g_ln.reshape(1, c), b_ln.reshape(1, c))


def _rope_tables(pos):
    half = HEAD_DIM // 2
    inv_freq = ROPE_THETA ** (-jnp.arange(half, dtype=F32) / half)
    ang = pos.astype(F32)[:, None] * inv_freq[None, :]
    cos, sin = jnp.cos(ang), jnp.sin(ang)
    cos64 = jnp.concatenate([cos, cos], axis=-1)
    sin64 = jnp.concatenate([-sin, sin], axis=-1)
    reps = V7X_LANES // HEAD_DIM
    return jnp.tile(cos64, (1, reps)), jnp.tile(sin64, (1, reps))


def kernel(x_prompt, x_sample, cache_k, cache_v, state_conv, g_mix_norm, w_in, sinks, w_attn_o,
           w_dw, b_dw, g_conv_ln, b_conv_ln, w_conv_o, w_out, g_ffn_norm, w_ffn_in, w_ffn_out,
           g_final):
    depth = w_in.shape[0]
    assert depth == 1, "single-layer trunk"
    t_p = x_prompt.shape[1]
    b_s, t_s, _ = x_sample.shape
    past_len = PAST_LEN
    m = t_p + b_s * t_s
    tm = TOKEN_TILE
    bw = WEIGHT_BLOCK
    sds = jax.ShapeDtypeStruct

    x = jnp.concatenate([x_prompt.reshape(t_p, D_MODEL), x_sample.reshape(b_s * t_s, D_MODEL)], axis=0)
    pos = jnp.concatenate([jnp.arange(t_p, dtype=jnp.int32),
                           jnp.tile(past_len + jnp.arange(t_s, dtype=jnp.int32), b_s)])
    cos_t, sin_t = _rope_tables(pos)
    rope_extras = [(cos_t, (tm, V7X_LANES), lambda j, i: (i, 0)),
                   (sin_t, (tm, V7X_LANES), lambda j, i: (i, 0))]

    l = 0
    w_in_l = w_in[l]
    h = _rmsnorm(x, g_mix_norm[l], BF16, "rmsnorm_mix")

    q_blk0 = 0
    kv_blk0 = Q_DIM // bw
    ua_blk0 = (Q_DIM + 2 * KV_DIM) // bw
    ub_blk0 = ua_blk0 + CONV_DIM // bw
    ga_blk0 = ub_blk0 + CONV_DIM // bw
    assert 2 * KV_DIM == bw

    (q,) = _wres_matmul(
        [(h, w_in_l, bw, [lambda j: q_blk0 + 2 * j, lambda j: q_blk0 + 2 * j + 1])],
        rope_extras,
        [(sds((m, Q_DIM), BF16), (tm, 2 * bw), lambda j, i: (i, j))],
        _epi_q, n_col_steps=Q_DIM // (2 * bw), name="proj_q")

    k_all, v_all = _wres_matmul(
        [(h, w_in_l, bw, [lambda j: kv_blk0])],
        rope_extras,
        [(sds((m, KV_DIM), F32), (tm, KV_DIM), lambda j, i: (i, 0)),
         (sds((m, KV_DIM), F32), (tm, KV_DIM), lambda j, i: (i, 0))],
        _epi_kv, n_col_steps=1, name="proj_kv")

    (cu,) = _wres_matmul(
        [(h, w_in_l, bw, [lambda j: ua_blk0 + j, lambda j: ub_blk0 + j])],
        [],
        [(sds((m, CONV_DIM), F32), (tm, bw), lambda j, i: (i, j))],
        _epi_glu, n_col_steps=CONV_DIM // bw, name="proj_glu")

    (gates,) = _wres_matmul(
        [(h, w_in_l, bw, [lambda j: ga_blk0 + 2 * j, lambda j: ga_blk0 + 2 * j + 1])],
        [],
        [(sds((m, 2 * D_MODEL), BF16), (tm, 2 * bw), lambda j, i: (i, j))],
        _epi_sigmoid, n_col_steps=2 * D_MODEL // (2 * bw), name="proj_gates")

    o_p = _attn_prompt(q, k_all, v_all, sinks[l], t_p)
    q_s = q[t_p:].reshape(b_s, t_s, N_KV_HEADS, N_GROUP, HEAD_DIM).transpose(0, 2, 3, 1, 4).reshape(
        b_s, N_KV_HEADS, N_GROUP * t_s, HEAD_DIM)
    k_s = k_all[t_p:].reshape(b_s, t_s, KV_DIM)
    v_s = v_all[t_p:].reshape(b_s, t_s, KV_DIM)
    n_past = cache_k.shape[2]
    o_s, k_win, v_win = _attn_sample(q_s, k_s, v_s,
                                     cache_k[l].reshape(b_s, n_past, KV_DIM),
                                     cache_v[l].reshape(b_s, n_past, KV_DIM), sinks[l])
    o_s = o_s.reshape(b_s, N_KV_HEADS, N_GROUP, t_s, HEAD_DIM).transpose(0, 3, 1, 2, 4).reshape(b_s * t_s, Q_DIM)
    o_all = jnp.concatenate([o_p, o_s], axis=0)

    c_p = _conv_prompt(cu, w_dw[l], b_dw[l], g_conv_ln[l], b_conv_ln[l], t_p)
    c_s, conv_state = _conv_sample(cu[t_p:].reshape(b_s, t_s, CONV_DIM), state_conv[l],
                                   w_dw[l], b_dw[l], g_conv_ln[l], b_conv_ln[l])
    c_all = jnp.concatenate([c_p, c_s.reshape(b_s * t_s, CONV_DIM)], axis=0)

    n_gate_blk = D_MODEL // (2 * bw)
    (mix,) = _wres_matmul(
        [(o_all, w_attn_o[l], bw, [lambda j: 2 * j, lambda j: 2 * j + 1]),
         (c_all, w_conv_o[l], bw, [lambda j: 2 * j, lambda j: 2 * j + 1])],
        [(gates, (tm, 2 * bw), lambda j, i: (i, j)),
         (gates, (tm, 2 * bw), lambda j, i: (i, j + n_gate_blk))],
        [(sds((m, D_MODEL), BF16), (tm, 2 * bw), lambda j, i: (i, j))],
        _epi_merge, n_col_steps=n_gate_blk, name="merge")

    (x1,) = _wres_matmul(
        [(mix, w_out[l], bw, [lambda j: 2 * j, lambda j: 2 * j + 1])],
        [(x, (tm, 2 * bw), lambda j, i: (i, j))],
        [(sds((m, D_MODEL), F32), (tm, 2 * bw), lambda j, i: (i, j))],
        _epi_residual, n_col_steps=D_MODEL // (2 * bw), name="out_proj", weight_buffers=1)

    h2 = _rmsnorm(x1, g_ffn_norm[l], BF16, "rmsnorm_ffn")
    fbw = 256
    n_ff_blk = D_FF // fbw
    (act,) = _wres_matmul(
        [(h2, w_ffn_in[l], fbw, [lambda j: j, lambda j: j + n_ff_blk])],
        [],
        [(sds((m, D_FF), BF16), (tm, fbw), lambda j, i: (i, j))],
        _epi_swiglu, n_col_steps=n_ff_blk, name="ffn_in")

    x2 = _ares_matmul_residual(act, w_ffn_out[l].astype(BF16), x1, tm=tm, tn=512, name="ffn_out")
    y = _rmsnorm(x2, g_final, F32, "rmsnorm_final")

    y_prompt = y[:t_p].reshape(x_prompt.shape)
    y_sample = y[t_p:].reshape(x_sample.shape)
    k_prompt = k_all[t_p - WINDOW:t_p].reshape(1, 1, WINDOW, N_KV_HEADS, HEAD_DIM)
    v_prompt = v_all[t_p - WINDOW:t_p].reshape(1, 1, WINDOW, N_KV_HEADS, HEAD_DIM)
    conv_prompt = cu[t_p - (CONV_WIDTH - 1):t_p].reshape(1, 1, CONV_WIDTH - 1, CONV_DIM)
    k_sample = k_win.reshape(1, b_s, WINDOW, N_KV_HEADS, HEAD_DIM)
    v_sample = v_win.reshape(1, b_s, WINDOW, N_KV_HEADS, HEAD_DIM)
    conv_sample = conv_state.reshape(1, b_s, CONV_WIDTH - 1, CONV_DIM)
    return (y_prompt, y_sample, k_prompt, v_prompt, conv_prompt, k_sample, v_sample, conv_sample)
```

```python
import collections
import functools

import jax
import jax.numpy as jnp
from jax import lax
from jax.experimental import pallas as pl
from jax.experimental.pallas import tpu as pltpu

D_MODEL = 4096
HEAD_DIM = 64
N_HEADS = 32
N_KV_HEADS = 4
N_GROUP = N_HEADS // N_KV_HEADS
WINDOW = 128
ROPE_THETA = 10000.0
CONV_DIM = 2048
CONV_WIDTH = 31
D_FF = 11008
PAST_LEN = 8192
Q_DIM = N_HEADS * HEAD_DIM
KV_DIM = N_KV_HEADS * HEAD_DIM
EPS = 1e-6
NEG = -1e30

V7X_LANES = 128
V7X_VMEM_LIMIT_BYTES = 60000 * 1024

F32 = jnp.float32
BF16 = jnp.bfloat16

TOKEN_TILE = 512
WEIGHT_BLOCK = 512
CAST_ROWS = 256
N_KV_PAIRS = N_KV_HEADS // 2
CHUNKS_PER_PAIR = N_GROUP

Stacked = collections.namedtuple("Stacked", ["prompt", "sample"])


def _params(*sem):
    return pltpu.CompilerParams(dimension_semantics=sem,
                                vmem_limit_bytes=V7X_VMEM_LIMIT_BYTES)


def _stacked_specs(op, block, imap, tm):
    if not isinstance(op, Stacked):
        return [op], [pl.BlockSpec(block, imap)]
    n_p = op.prompt.shape[0] // tm
    assert op.prompt.shape[0] % tm == 0 and op.sample.shape[0] % tm == 0

    def p_map(*ids):
        r, c = imap(*ids)
        return (jnp.minimum(r, n_p - 1), c)

    def s_map(*ids):
        r, c = imap(*ids)
        return (jnp.maximum(r - n_p, 0), c)

    return [op.prompt, op.sample], [pl.BlockSpec(block, p_map), pl.BlockSpec(block, s_map)]


def _n_prompt_tiles(op, tm):
    return op.prompt.shape[0] // tm if isinstance(op, Stacked) else None


def _load(refs, n_p, row_tile):
    if n_p is None:
        return refs[0][...]
    return jnp.where(row_tile < n_p, refs[0][...], refs[1][...])


def _rmsnorm_kernel(*refs, n_p):
    x_refs, (g_ref, o_ref) = refs[:-2], refs[-2:]
    x = _load(x_refs, n_p, pl.program_id(0))
    y = x * lax.rsqrt(jnp.mean(x * x, axis=-1, keepdims=True) + EPS) * g_ref[...]
    o_ref[...] = y.astype(o_ref.dtype)


def _rmsnorm(x, g, out_dtype, name, *, row_tile0=0, n_tiles=None, tm=256):
    d = g.shape[0]
    rows = (x.prompt.shape[0] + x.sample.shape[0]) if isinstance(x, Stacked) else x.shape[0]
    n_tiles = rows // tm if n_tiles is None else n_tiles
    arrays, specs = _stacked_specs(x, (tm, d), lambda i: (i + row_tile0, 0), tm)
    return pl.pallas_call(
        functools.partial(_rmsnorm_kernel, n_p=_n_prompt_tiles(x, tm)),
        grid=(n_tiles,),
        in_specs=specs + [pl.BlockSpec((1, d), lambda i: (0, 0))],
        out_specs=pl.BlockSpec((tm, d), lambda i: (i, 0)),
        out_shape=jax.ShapeDtypeStruct((n_tiles * tm, d), out_dtype),
        compiler_params=_params("arbitrary"),
        name=name,
    )(*arrays, g.reshape(1, d))


def _cast_plain(w_refs, wbf_ref):
    k, bw = w_refs[0].shape

    def body(c, carry):
        r = pl.multiple_of(c * CAST_ROWS, CAST_ROWS)
        for b, w_ref in enumerate(w_refs):
            wbf_ref[pl.ds(r, CAST_ROWS), b * bw:(b + 1) * bw] = w_ref[pl.ds(r, CAST_ROWS), :].astype(BF16)
        return carry

    lax.fori_loop(0, k // CAST_ROWS, body, 0)


def _cast_interleave_heads(w_refs, wbf_ref):
    w_even, w_odd = w_refs
    k, bw = w_even.shape

    def body(c, carry):
        r = pl.multiple_of(c * CAST_ROWS, CAST_ROWS)
        a = w_even[pl.ds(r, CAST_ROWS), :]
        b = w_odd[pl.ds(r, CAST_ROWS), :]
        pieces = []
        for g in range(bw // HEAD_DIM):
            pieces += [a[:, g * HEAD_DIM:(g + 1) * HEAD_DIM], b[:, g * HEAD_DIM:(g + 1) * HEAD_DIM]]
        wbf_ref[pl.ds(r, CAST_ROWS), :] = jnp.concatenate(pieces, axis=1).astype(BF16)
        return carry

    lax.fori_loop(0, k // CAST_ROWS, body, 0)


def _interleaved_head(slot):
    chunk, parity = divmod(slot, 2)
    pair, g = divmod(chunk, CHUNKS_PER_PAIR)
    return (2 * pair + parity) * N_GROUP + g


def _cast_permute_head_rows(w_refs, wbf_ref):
    bw = w_refs[0].shape[1]
    for b, w_ref in enumerate(w_refs):
        for slot in range(N_HEADS):
            h = _interleaved_head(slot)
            wbf_ref[slot * HEAD_DIM:(slot + 1) * HEAD_DIM, b * bw:(b + 1) * bw] = (
                w_ref[h * HEAD_DIM:(h + 1) * HEAD_DIM, :].astype(BF16))


def _wres_kernel(*refs, layout, casts, epilogue):
    a_counts, w_counts, extra_counts, n_out, a_np, extra_np = layout
    pos = 0

    def take(counts):
        nonlocal pos
        groups = []
        for n in counts:
            groups.append(refs[pos:pos + n])
            pos += n
        return groups

    a_groups = take(a_counts)
    w_groups = take(w_counts)
    extra_groups = take(extra_counts)
    out_refs = refs[pos:pos + n_out]
    wbf_refs = refs[pos + n_out:]
    i = pl.program_id(1)

    @pl.when(i == 0)
    def _():
        for cast, w_refs, wbf_ref in zip(casts, w_groups, wbf_refs):
            cast(w_refs, wbf_ref)

    accs = [jnp.dot(_load(a_refs, n_p, i), wbf_ref[...], preferred_element_type=F32)
            for a_refs, n_p, wbf_ref in zip(a_groups, a_np, wbf_refs)]
    extras = [_load(e_refs, n_p, i) for e_refs, n_p in zip(extra_groups, extra_np)]
    for o_ref, o in zip(out_refs, epilogue(accs, extras)):
        o_ref[...] = o.astype(o_ref.dtype)


def _wres_matmul(terms, extras, outs, epilogue, *, n_col_steps, name, tm=TOKEN_TILE,
                 weight_buffers=2):
    a0 = terms[0][0]
    m = (a0.prompt.shape[0] + a0.sample.shape[0]) if isinstance(a0, Stacked) else a0.shape[0]
    in_arrays, in_specs, scratch, casts = [], [], [], []
    a_counts, w_counts, extra_counts, a_np, extra_np = [], [], [], [], []
    for a, w, bw, col_fns, cast in terms:
        k = w.shape[0]
        arrs, specs = _stacked_specs(a, (tm, k), lambda j, i: (i, 0), tm)
        in_arrays += arrs
        in_specs += specs
        a_counts.append(len(arrs))
        a_np.append(_n_prompt_tiles(a, tm))
    for a, w, bw, col_fns, cast in terms:
        k = w.shape[0]
        for fn in col_fns:
            in_arrays.append(w)
            in_specs.append(pl.BlockSpec((k, bw), functools.partial(lambda j, i, fn: (0, fn(j)), fn=fn),
                                         pipeline_mode=pl.Buffered(weight_buffers)))
        w_counts.append(len(col_fns))
        casts.append(cast)
        scratch.append(pltpu.VMEM((k, bw * len(col_fns)), BF16))
    for op, blk, imap in extras:
        arrs, specs = _stacked_specs(op, blk, imap, tm)
        in_arrays += arrs
        in_specs += specs
        extra_counts.append(len(arrs))
        extra_np.append(_n_prompt_tiles(op, tm))
    layout = (tuple(a_counts), tuple(w_counts), tuple(extra_counts), len(outs),
              tuple(a_np), tuple(extra_np))
    return pl.pallas_call(
        functools.partial(_wres_kernel, layout=layout, casts=tuple(casts), epilogue=epilogue),
        grid=(n_col_steps, m // tm),
        in_specs=in_specs,
        out_specs=[pl.BlockSpec(blk, imap) for _, blk, imap in outs],
        out_shape=[s for s, _, _ in outs],
        scratch_shapes=scratch,
        compiler_params=_params("arbitrary", "arbitrary"),
        name=name,
    )(*in_arrays)


def _ares_kernel(a_ref, w_ref, x_ref, o_ref):
    acc = jnp.dot(a_ref[...], w_ref[...], preferred_element_type=F32)
    o_ref[...] = x_ref[...] + acc


def _ares_matmul_residual(a, w_bf, x, *, tm, tn, name):
    m, k = a.shape
    n = w_bf.shape[1]
    return pl.pallas_call(
        _ares_kernel,
        grid=(m // tm, n // tn),
        in_specs=[pl.BlockSpec((tm, k), lambda i, j: (i, 0)),
                  pl.BlockSpec((k, tn), lambda i, j: (0, j)),
                  pl.BlockSpec((tm, tn), lambda i, j: (i, j))],
        out_specs=pl.BlockSpec((tm, tn), lambda i, j: (i, j)),
        out_shape=jax.ShapeDtypeStruct((m, n), F32),
        compiler_params=_params("arbitrary", "arbitrary"),
        name=name,
    )(a, w_bf, x)


def _rope(x, cos128, sin128):
    n = x.shape[1]
    reps = n // V7X_LANES
    c = jnp.concatenate([cos128] * reps, axis=1) if reps > 1 else cos128
    s = jnp.concatenate([sin128] * reps, axis=1) if reps > 1 else sin128
    lane = lax.broadcasted_iota(jnp.int32, x.shape, 1)
    first_half = (lane & (HEAD_DIM - 1)) < (HEAD_DIM // 2)
    partner = jnp.where(first_half,
                        pltpu.roll(x, n - HEAD_DIM // 2, axis=1),
                        pltpu.roll(x, HEAD_DIM // 2, axis=1))
    return x * c + partner * s


def _epi_q(accs, extras):
    cos, sin = extras
    return [_rope(accs[0], cos, sin)]


def _epi_kv(accs, extras):
    cos, sin = extras
    acc = accs[0]
    return [_rope(acc[:, :KV_DIM], cos, sin), acc[:, KV_DIM:]]


def _epi_glu(accs, extras):
    acc = accs[0]
    half = acc.shape[1] // 2
    return [acc[:, :half] * jax.nn.sigmoid(acc[:, half:])]


def _epi_sigmoid(accs, extras):
    return [jax.nn.sigmoid(accs[0])]


def _epi_merge(accs, extras):
    g_a, g_c = extras
    return [g_a.astype(F32) * accs[0] + g_c.astype(F32) * accs[1]]


def _epi_residual(accs, extras):
    return [extras[0] + accs[0]]


def _epi_swiglu_and_cast(accs, extras):
    acc = accs[0]
    half = acc.shape[1] // 2
    return [jax.nn.silu(acc[:, :half]) * acc[:, half:], extras[0]]


def _attend(qs, kx, vx_ones, valid, sk, n_rep):
    r, nk = qs.shape[0], kx.shape[0]
    s = lax.dot_general(qs, kx, (((1,), (1,)), ((), ())), preferred_element_type=F32)
    s = jnp.where(valid[None], s.reshape(n_rep, r // n_rep, nk), NEG).reshape(r, nk)
    m = jnp.maximum(jnp.max(s, axis=-1, keepdims=True), sk)
    p = jnp.exp(s - m).astype(BF16)
    ol = jnp.dot(p, vx_ones, preferred_element_type=F32)
    return ol[:, :V7X_LANES] / (ol[:, V7X_LANES:] + jnp.exp(sk - m))


def _attend_all_heads(q_ref, o_ref, sink_ref, k_of_pair, v_of_pair, valid):
    rows = q_ref.shape[0]
    nk = valid.shape[1]
    lane = lax.broadcasted_iota(jnp.int32, (nk, V7X_LANES), 1)
    ones = jnp.ones((nk, V7X_LANES), BF16)
    scale = HEAD_DIM ** -0.5
    for pair in range(N_KV_PAIRS):
        kcol = k_of_pair(pair) * scale
        vcol = v_of_pair(pair)
        c0 = pair * CHUNKS_PER_PAIR
        qs = jnp.concatenate([q_ref[:, (c0 + g) * V7X_LANES:(c0 + g + 1) * V7X_LANES]
                              for g in range(CHUNKS_PER_PAIR)], axis=0)
        o_pair = None
        for parity in range(2):
            half = (lane < HEAD_DIM) if parity == 0 else (lane >= HEAD_DIM)
            kx = jnp.where(half, kcol, 0.0).astype(BF16)
            vx = jnp.concatenate([jnp.where(half, vcol, 0.0).astype(BF16), ones], axis=1)
            o = _attend(qs, kx, vx, valid, sink_ref[2 * pair + parity][:, :1], CHUNKS_PER_PAIR)
            o_pair = o if o_pair is None else o_pair + o
        for g in range(CHUNKS_PER_PAIR):
            o_ref[:, (c0 + g) * V7X_LANES:(c0 + g + 1) * V7X_LANES] = (
                o_pair[g * rows:(g + 1) * rows].astype(o_ref.dtype))


def _sink_table(sinks, rows):
    per_row = jnp.repeat(sinks.astype(F32).reshape(N_KV_HEADS, N_GROUP), rows, axis=1)
    return jnp.broadcast_to(per_row[:, :, None], (N_KV_HEADS, N_GROUP * rows, V7X_LANES))


def _attn_prompt_kernel(sink_ref, q_ref, kc_ref, kp_ref, vc_ref, vp_ref, o_ref):
    n = pl.program_id(0)
    blk = q_ref.shape[0]
    row = lax.broadcasted_iota(jnp.int32, (blk, 2 * blk), 0)
    col = lax.broadcasted_iota(jnp.int32, (blk, 2 * blk), 1)
    diff = row + blk - col
    valid = (diff >= 0) & (diff < WINDOW) & ((n > 0) | (col >= blk))

    def both_blocks(prev_ref, cur_ref):
        return lambda pair: jnp.concatenate(
            [prev_ref[:, pair * V7X_LANES:(pair + 1) * V7X_LANES],
             cur_ref[:, pair * V7X_LANES:(pair + 1) * V7X_LANES]], axis=0)

    _attend_all_heads(q_ref, o_ref, sink_ref, both_blocks(kp_ref, kc_ref), both_blocks(vp_ref, vc_ref), valid)


def _attn_prompt(q, k, v, sinks, t_prompt):
    blk = WINDOW
    nb = t_prompt // blk
    cur = lambda n: (n, 0)
    prev = lambda n: (jnp.maximum(n - 1, 0), 0)
    return pl.pallas_call(
        _attn_prompt_kernel,
        grid=(nb,),
        in_specs=[pl.BlockSpec((N_KV_HEADS, N_GROUP * blk, V7X_LANES), lambda n: (0, 0, 0)),
                  pl.BlockSpec((blk, Q_DIM), cur),
                  pl.BlockSpec((blk, KV_DIM), cur),
                  pl.BlockSpec((blk, KV_DIM), prev),
                  pl.BlockSpec((blk, KV_DIM), cur),
                  pl.BlockSpec((blk, KV_DIM), prev)],
        out_specs=pl.BlockSpec((blk, Q_DIM), cur),
        out_shape=jax.ShapeDtypeStruct((t_prompt, Q_DIM), BF16),
        compiler_params=_params("arbitrary"),
        name="attn_prompt",
    )(_sink_table(sinks, blk), q, k, k, v, v)


SAMPLE_BATCH_BLOCK = 8
NEW_KEY_PAD = 128


def _attn_sample_kernel(sink_ref, q_ref, kn_ref, vn_ref, ck_ref, cv_ref, o_ref, ko_ref, vo_ref):
    bb, n_past = ck_ref.shape[0], ck_ref.shape[1]
    rows = q_ref.shape[0]
    t_new = rows // bb
    n_cache = bb * n_past
    nk = n_cache + NEW_KEY_PAD
    lg_t, lg_p = t_new.bit_length() - 1, n_past.bit_length() - 1

    for b in range(bb):
        ko_ref[b, 0:n_past - t_new, :] = ck_ref[b, t_new:n_past, :]
        ko_ref[b, n_past - t_new:n_past, :] = kn_ref[b * t_new:(b + 1) * t_new, :]
        vo_ref[b, 0:n_past - t_new, :] = cv_ref[b, t_new:n_past, :]
        vo_ref[b, n_past - t_new:n_past, :] = vn_ref[b * t_new:(b + 1) * t_new, :]

    r = lax.broadcasted_iota(jnp.int32, (rows, nk), 0)
    c = lax.broadcasted_iota(jnp.int32, (rows, nk), 1)
    b_r, t = r >> lg_t, r & (t_new - 1)
    d_cache = t + n_past - (c & (n_past - 1))
    valid_cache = (c < n_cache) & ((c >> lg_p) == b_r) & (d_cache >= 0) & (d_cache < WINDOW)
    cn = c - n_cache
    d_new = t - (cn & (t_new - 1))
    valid_new = (cn >= 0) & (cn < rows) & ((cn >> lg_t) == b_r) & (d_new >= 0) & (d_new < WINDOW)
    valid = valid_cache | valid_new

    pad = jnp.zeros((NEW_KEY_PAD - rows, V7X_LANES), F32)

    def keys(cache_ref, new_ref):
        def of_pair(pair):
            ls = slice(pair * V7X_LANES, (pair + 1) * V7X_LANES)
            return jnp.concatenate([cache_ref[:, :, ls].reshape(n_cache, V7X_LANES), new_ref[:, ls], pad], axis=0)
        return of_pair

    _attend_all_heads(q_ref, o_ref, sink_ref, keys(ck_ref, kn_ref), keys(cv_ref, vn_ref), valid)


def _attn_sample(q, k, v, cache_k, cache_v, sinks, row0, t_new):
    b, n_past, _ = cache_k.shape
    assert n_past == WINDOW and t_new & (t_new - 1) == 0
    bb = SAMPLE_BATCH_BLOCK
    rows = bb * t_new
    assert row0 % rows == 0 and rows <= NEW_KEY_PAD
    tok = lambda i: (row0 // rows + i, 0)
    i3 = lambda i: (i, 0, 0)
    return pl.pallas_call(
        _attn_sample_kernel,
        grid=(b // bb,),
        in_specs=[pl.BlockSpec((N_KV_HEADS, N_GROUP * rows, V7X_LANES), lambda i: (0, 0, 0)),
                  pl.BlockSpec((rows, Q_DIM), tok),
                  pl.BlockSpec((rows, KV_DIM), tok),
                  pl.BlockSpec((rows, KV_DIM), tok),
                  pl.BlockSpec((bb, n_past, KV_DIM), i3),
                  pl.BlockSpec((bb, n_past, KV_DIM), i3)],
        out_specs=[pl.BlockSpec((rows, Q_DIM), lambda i: (i, 0)),
                   pl.BlockSpec((bb, n_past, KV_DIM), i3),
                   pl.BlockSpec((bb, n_past, KV_DIM), i3)],
        out_shape=[jax.ShapeDtypeStruct((b * t_new, Q_DIM), BF16),
                   jax.ShapeDtypeStruct((b, n_past, KV_DIM), F32),
                   jax.ShapeDtypeStruct((b, n_past, KV_DIM), F32)],
        compiler_params=_params("arbitrary"),
        name="attn_sample",
    )(_sink_table(sinks, rows), q, k, v, cache_k, cache_v)


def _ln_swish(c, g, b):
    mu = jnp.mean(c, axis=-1, keepdims=True)
    var = jnp.mean(jnp.square(c - mu), axis=-1, keepdims=True)
    y = (c - mu) * lax.rsqrt(var + EPS) * g + b
    return jax.nn.silu(y)


CONV_HALO = 32


def _conv_prompt_kernel(cur_ref, halo_ref, w_ref, b_ref, g_ref, bl_ref, o_ref, xs_ref, c_ref):
    i = pl.program_id(0)
    tt = cur_ref.shape[0]
    xs_ref[0:CONV_HALO, :] = jnp.where(i > 0, halo_ref[...], 0.0)
    xs_ref[CONV_HALO:CONV_HALO + tt, :] = cur_ref[...]
    off = CONV_HALO - (CONV_WIDTH - 1)
    for cc in range(cur_ref.shape[1] // V7X_LANES):
        ls = slice(cc * V7X_LANES, (cc + 1) * V7X_LANES)
        acc = xs_ref[off:off + tt, ls] * w_ref[0:1, ls]
        for w in range(1, CONV_WIDTH):
            acc = acc + xs_ref[off + w:off + w + tt, ls] * w_ref[w:w + 1, ls]
        c_ref[:, ls] = acc + b_ref[:, ls]
    o_ref[...] = _ln_swish(c_ref[...], g_ref[...], bl_ref[...]).astype(o_ref.dtype)


def _conv_prompt(cu, w_dw, b_dw, g_ln, b_ln, t_prompt):
    tt = 256
    c = cu.shape[1]
    per = tt // CONV_HALO
    row1 = lambda i: (0, 0)
    return pl.pallas_call(
        _conv_prompt_kernel,
        grid=(t_prompt // tt,),
        in_specs=[pl.BlockSpec((tt, c), lambda i: (i, 0)),
                  pl.BlockSpec((CONV_HALO, c), lambda i: (jnp.maximum(i * per - 1, 0), 0)),
                  pl.BlockSpec((CONV_WIDTH, c), row1),
                  pl.BlockSpec((1, c), row1),
                  pl.BlockSpec((1, c), row1),
                  pl.BlockSpec((1, c), row1)],
        out_specs=pl.BlockSpec((tt, c), lambda i: (i, 0)),
        out_shape=jax.ShapeDtypeStruct((t_prompt, c), BF16),
        scratch_shapes=[pltpu.VMEM((CONV_HALO + tt, c), F32), pltpu.VMEM((tt, c), F32)],
        compiler_params=_params("arbitrary"),
        name="conv_prompt",
    )(cu, cu, w_dw, b_dw.reshape(1, c), g_ln.reshape(1, c), b_ln.reshape(1, c))


def _conv_sample_kernel(st_ref, cu_ref, w_ref, b_ref, g_ref, bl_ref, o_ref, so_ref, xs_ref):
    bb, n_hist, c = st_ref.shape
    t_new = cu_ref.shape[0] // bb
    for b in range(bb):
        xs_ref[0:n_hist, :] = st_ref[b]
        xs_ref[n_hist:n_hist + t_new, :] = cu_ref[b * t_new:(b + 1) * t_new, :]
        so_ref[b] = xs_ref[t_new:t_new + n_hist, :]
        acc = xs_ref[0:t_new, :] * w_ref[0:1, :]
        for w in range(1, CONV_WIDTH):
            acc = acc + xs_ref[w:w + t_new, :] * w_ref[w:w + 1, :]
        o_ref[b * t_new:(b + 1) * t_new, :] = _ln_swish(
            acc + b_ref[...], g_ref[...], bl_ref[...]).astype(o_ref.dtype)


def _conv_sample(cu, state, w_dw, b_dw, g_ln, b_ln, row0, t_new):
    b, n_hist, c = state.shape
    bb = SAMPLE_BATCH_BLOCK
    rows = bb * t_new
    assert row0 % rows == 0
    i3 = lambda i: (i, 0, 0)
    row1 = lambda i: (0, 0)
    return pl.pallas_call(
        _conv_sample_kernel,
        grid=(b // bb,),
        in_specs=[pl.BlockSpec((bb, n_hist, c), i3),
                  pl.BlockSpec((rows, c), lambda i: (row0 // rows + i, 0)),
                  pl.BlockSpec((CONV_WIDTH, c), row1),
                  pl.BlockSpec((1, c), row1),
                  pl.BlockSpec((1, c), row1),
                  pl.BlockSpec((1, c), row1)],
        out_specs=[pl.BlockSpec((rows, c), lambda i: (i, 0)),
                   pl.BlockSpec((bb, n_hist, c), i3)],
        out_shape=[jax.ShapeDtypeStruct((b * t_new, c), BF16),
                   jax.ShapeDtypeStruct((b, n_hist, c), F32)],
        scratch_shapes=[pltpu.VMEM((n_hist + t_new + 6, c), F32)],
        compiler_params=_params("arbitrary"),
        name="conv_sample",
    )(state, cu, w_dw, b_dw.reshape(1, c), g_ln.reshape(1, c), b_ln.reshape(1, c))


def _rope_tables(pos):
    half = HEAD_DIM // 2
    inv_freq = ROPE_THETA ** (-jnp.arange(half, dtype=F32) / half)
    ang = pos.astype(F32)[:, None] * inv_freq[None, :]
    cos, sin = jnp.cos(ang), jnp.sin(ang)
    cos64 = jnp.concatenate([cos, cos], axis=-1)
    sin64 = jnp.concatenate([-sin, sin], axis=-1)
    reps = V7X_LANES // HEAD_DIM
    return jnp.tile(cos64, (1, reps)), jnp.tile(sin64, (1, reps))


FFN_TOKEN_TILE = 1088
FFN_BLOCK = 256


def kernel(x_prompt, x_sample, cache_k, cache_v, state_conv, g_mix_norm, w_in, sinks, w_attn_o,
           w_dw, b_dw, g_conv_ln, b_conv_ln, w_conv_o, w_out, g_ffn_norm, w_ffn_in, w_ffn_out,
           g_final):
    depth = w_in.shape[0]
    assert depth == 1, "single-layer trunk"
    t_p = x_prompt.shape[1]
    b_s, t_s, _ = x_sample.shape
    m = t_p + b_s * t_s
    tm = TOKEN_TILE
    bw = WEIGHT_BLOCK
    sds = jax.ShapeDtypeStruct
    plain = _cast_plain

    x = Stacked(x_prompt.reshape(t_p, D_MODEL), x_sample.reshape(b_s * t_s, D_MODEL))
    pos = jnp.concatenate([jnp.arange(t_p, dtype=jnp.int32),
                           jnp.tile(PAST_LEN + jnp.arange(t_s, dtype=jnp.int32), b_s)])
    cos_t, sin_t = _rope_tables(pos)
    rope_extras = [(cos_t, (tm, V7X_LANES), lambda j, i: (i, 0)),
                   (sin_t, (tm, V7X_LANES), lambda j, i: (i, 0))]

    l = 0
    w_in_l = w_in[l]
    h = _rmsnorm(x, g_mix_norm[l], BF16, "rmsnorm_mix")

    kv_blk0 = Q_DIM // bw
    ua_blk0 = (Q_DIM + 2 * KV_DIM) // bw
    ub_blk0 = ua_blk0 + CONV_DIM // bw
    ga_blk0 = ub_blk0 + CONV_DIM // bw
    assert 2 * KV_DIM == bw and N_GROUP * HEAD_DIM == bw

    (q,) = _wres_matmul(
        [(h, w_in_l, bw, [lambda j: 2 * j, lambda j: 2 * j + 1], _cast_interleave_heads)],
        rope_extras,
        [(sds((m, Q_DIM), BF16), (tm, 2 * bw), lambda j, i: (i, j))],
        _epi_q, n_col_steps=N_KV_PAIRS, name="proj_q")

    k_all, v_all = _wres_matmul(
        [(h, w_in_l, bw, [lambda j: kv_blk0], plain)],
        rope_extras,
        [(sds((m, KV_DIM), F32), (tm, KV_DIM), lambda j, i: (i, 0)),
         (sds((m, KV_DIM), F32), (tm, KV_DIM), lambda j, i: (i, 0))],
        _epi_kv, n_col_steps=1, name="proj_kv")

    (cu,) = _wres_matmul(
        [(h, w_in_l, bw, [lambda j: ua_blk0 + j, lambda j: ub_blk0 + j], plain)],
        [],
        [(sds((m, CONV_DIM), F32), (tm, bw), lambda j, i: (i, j))],
        _epi_glu, n_col_steps=CONV_DIM // bw, name="proj_glu")

    (gates,) = _wres_matmul(
        [(h, w_in_l, bw, [lambda j: ga_blk0 + 2 * j, lambda j: ga_blk0 + 2 * j + 1], plain)],
        [],
        [(sds((m, 2 * D_MODEL), BF16), (tm, 2 * bw), lambda j, i: (i, j))],
        _epi_sigmoid, n_col_steps=2 * D_MODEL // (2 * bw), name="proj_gates")

    n_past = cache_k.shape[2]
    o_p = _attn_prompt(q, k_all, v_all, sinks[l], t_p)
    o_s, k_win, v_win = _attn_sample(q, k_all, v_all,
                                     cache_k[l].reshape(b_s, n_past, KV_DIM),
                                     cache_v[l].reshape(b_s, n_past, KV_DIM), sinks[l], t_p, t_s)

    c_p = _conv_prompt(cu, w_dw[l], b_dw[l], g_conv_ln[l], b_conv_ln[l], t_p)
    c_s, conv_state = _conv_sample(cu, state_conv[l], w_dw[l], b_dw[l], g_conv_ln[l], b_conv_ln[l],
                                   t_p, t_s)

    n_gate_blk = D_MODEL // (2 * bw)
    two_blocks = [lambda j: 2 * j, lambda j: 2 * j + 1]
    (mix,) = _wres_matmul(
        [(Stacked(o_p, o_s), w_attn_o[l], bw, two_blocks, _cast_permute_head_rows),
         (Stacked(c_p, c_s), w_conv_o[l], bw, two_blocks, plain)],
        [(gates, (tm, 2 * bw), lambda j, i: (i, j)),
         (gates, (tm, 2 * bw), lambda j, i: (i, j + n_gate_blk))],
        [(sds((m, D_MODEL), BF16), (tm, 2 * bw), lambda j, i: (i, j))],
        _epi_merge, n_col_steps=n_gate_blk, name="merge", weight_buffers=1)

    (x1,) = _wres_matmul(
        [(mix, w_out[l], bw, two_blocks, plain)],
        [(x, (tm, 2 * bw), lambda j, i: (i, j))],
        [(sds((m, D_MODEL), F32), (tm, 2 * bw), lambda j, i: (i, j))],
        _epi_residual, n_col_steps=D_MODEL // (2 * bw), name="out_proj", weight_buffers=1)

    h2 = _rmsnorm(x1, g_ffn_norm[l], BF16, "rmsnorm_ffn")
    ftm = FFN_TOKEN_TILE
    n_ff_blk = D_FF // FFN_BLOCK
    n_ff_steps = n_ff_blk * (m // ftm)
    slab = D_FF // n_ff_steps
    assert m % ftm == 0 and slab * n_ff_steps == D_FF and slab % 16 == 0
    slab_map = lambda j, i: (j * (m // ftm) + i, 0)
    act, w_ffn_out_bf = _wres_matmul(
        [(h2, w_ffn_in[l], FFN_BLOCK, [lambda j: j, lambda j: j + n_ff_blk], plain)],
        [(w_ffn_out[l], (slab, D_MODEL), slab_map)],
        [(sds((m, D_FF), BF16), (ftm, FFN_BLOCK), lambda j, i: (i, j)),
         (sds((D_FF, D_MODEL), BF16), (slab, D_MODEL), slab_map)],
        _epi_swiglu_and_cast, n_col_steps=n_ff_blk, name="ffn_in", tm=ftm)

    x2 = _ares_matmul_residual(act, w_ffn_out_bf, x1, tm=tm, tn=512, name="ffn_out")
    norm_tile = 256
    y_p = _rmsnorm(x2, g_final, F32, "rmsnorm_final_prompt", row_tile0=0, n_tiles=t_p // norm_tile,
                   tm=norm_tile)
    y_s = _rmsnorm(x2, g_final, F32, "rmsnorm_final_sample", row_tile0=t_p // norm_tile,
                   n_tiles=(m - t_p) // norm_tile, tm=norm_tile)

    y_prompt = y_p.reshape(x_prompt.shape)
    y_sample = y_s.reshape(x_sample.shape)
    k_prompt = k_all[t_p - WINDOW:t_p].reshape(1, 1, WINDOW, N_KV_HEADS, HEAD_DIM)
    v_prompt = v_all[t_p - WINDOW:t_p].reshape(1, 1, WINDOW, N_KV_HEADS, HEAD_DIM)
    conv_prompt = cu[t_p - (CONV_WIDTH - 1):t_p].reshape(1, 1, CONV_WIDTH - 1, CONV_DIM)
    k_sample = k_win.reshape(1, b_s, WINDOW, N_KV_HEADS, HEAD_DIM)
    v_sample = v_win.reshape(1, b_s, WINDOW, N_KV_HEADS, HEAD_DIM)
    conv_sample = conv_state.reshape(1, b_s, CONV_WIDTH - 1, CONV_DIM)
    return (y_prompt, y_sample, k_prompt, v_prompt, conv_prompt, k_sample, v_sample, conv_sample)
```

```python
import collections
import functools

import jax
import jax.numpy as jnp
from jax import lax
from jax.experimental import pallas as pl
from jax.experimental.pallas import tpu as pltpu

D_MODEL = 4096
HEAD_DIM = 64
N_HEADS = 32
N_KV_HEADS = 4
N_GROUP = N_HEADS // N_KV_HEADS
WINDOW = 128
ROPE_THETA = 10000.0
CONV_DIM = 2048
CONV_WIDTH = 31
D_FF = 11008
PAST_LEN = 8192
Q_DIM = N_HEADS * HEAD_DIM
KV_DIM = N_KV_HEADS * HEAD_DIM
EPS = 1e-6
NEG = -1e30

V7X_LANES = 128
V7X_VMEM_LIMIT_BYTES = 60000 * 1024

F32 = jnp.float32
BF16 = jnp.bfloat16

TOKEN_TILE = 512
WEIGHT_BLOCK = 512
CAST_ROWS = 256
N_KV_PAIRS = N_KV_HEADS // 2
CHUNKS_PER_PAIR = N_GROUP

Stacked = collections.namedtuple("Stacked", ["prompt", "sample"])


def _params(*sem):
    return pltpu.CompilerParams(dimension_semantics=sem,
                                vmem_limit_bytes=V7X_VMEM_LIMIT_BYTES)


def _stacked_specs(op, block, imap, tm):
    if not isinstance(op, Stacked):
        return [op], [pl.BlockSpec(block, imap)]
    n_p = op.prompt.shape[0] // tm
    assert op.prompt.shape[0] % tm == 0 and op.sample.shape[0] % tm == 0

    def p_map(*ids):
        r, c = imap(*ids)
        return (jnp.minimum(r, n_p - 1), c)

    def s_map(*ids):
        r, c = imap(*ids)
        return (jnp.maximum(r - n_p, 0), c)

    return [op.prompt, op.sample], [pl.BlockSpec(block, p_map), pl.BlockSpec(block, s_map)]


def _n_prompt_tiles(op, tm):
    return op.prompt.shape[0] // tm if isinstance(op, Stacked) else None


def _load(refs, n_p, row_tile):
    if n_p is None:
        return refs[0][...]
    return jnp.where(row_tile < n_p, refs[0][...], refs[1][...])


def _rmsnorm_kernel(*refs, n_p):
    x_refs, (g_ref, o_ref) = refs[:-2], refs[-2:]
    x = _load(x_refs, n_p, pl.program_id(0))
    y = x * lax.rsqrt(jnp.mean(x * x, axis=-1, keepdims=True) + EPS) * g_ref[...]
    o_ref[...] = y.astype(o_ref.dtype)


def _rmsnorm(x, g, out_dtype, name, *, row_tile0=0, n_tiles=None, tm=256):
    d = g.shape[0]
    rows = (x.prompt.shape[0] + x.sample.shape[0]) if isinstance(x, Stacked) else x.shape[0]
    n_tiles = rows // tm if n_tiles is None else n_tiles
    arrays, specs = _stacked_specs(x, (tm, d), lambda i: (i + row_tile0, 0), tm)
    return pl.pallas_call(
        functools.partial(_rmsnorm_kernel, n_p=_n_prompt_tiles(x, tm)),
        grid=(n_tiles,),
        in_specs=specs + [pl.BlockSpec((1, d), lambda i: (0, 0))],
        out_specs=pl.BlockSpec((tm, d), lambda i: (i, 0)),
        out_shape=jax.ShapeDtypeStruct((n_tiles * tm, d), out_dtype),
        compiler_params=_params("arbitrary"),
        name=name,
    )(*arrays, g.reshape(1, d))


def _cast_plain(w_refs, wbf_ref):
    k, bw = w_refs[0].shape

    def body(c, carry):
        r = pl.multiple_of(c * CAST_ROWS, CAST_ROWS)
        for b, w_ref in enumerate(w_refs):
            wbf_ref[pl.ds(r, CAST_ROWS), b * bw:(b + 1) * bw] = w_ref[pl.ds(r, CAST_ROWS), :].astype(BF16)
        return carry

    lax.fori_loop(0, k // CAST_ROWS, body, 0)


def _cast_interleave_heads(w_refs, wbf_ref):
    w_even, w_odd = w_refs
    k, bw = w_even.shape

    def body(c, carry):
        r = pl.multiple_of(c * CAST_ROWS, CAST_ROWS)
        a = w_even[pl.ds(r, CAST_ROWS), :]
        b = w_odd[pl.ds(r, CAST_ROWS), :]
        pieces = []
        for g in range(bw // HEAD_DIM):
            pieces += [a[:, g * HEAD_DIM:(g + 1) * HEAD_DIM], b[:, g * HEAD_DIM:(g + 1) * HEAD_DIM]]
        wbf_ref[pl.ds(r, CAST_ROWS), :] = jnp.concatenate(pieces, axis=1).astype(BF16)
        return carry

    lax.fori_loop(0, k // CAST_ROWS, body, 0)


def _interleaved_head(slot):
    chunk, parity = divmod(slot, 2)
    pair, g = divmod(chunk, CHUNKS_PER_PAIR)
    return (2 * pair + parity) * N_GROUP + g


def _cast_permute_head_rows(w_refs, wbf_ref):
    bw = w_refs[0].shape[1]
    for b, w_ref in enumerate(w_refs):
        for slot in range(N_HEADS):
            h = _interleaved_head(slot)
            wbf_ref[slot * HEAD_DIM:(slot + 1) * HEAD_DIM, b * bw:(b + 1) * bw] = (
                w_ref[h * HEAD_DIM:(h + 1) * HEAD_DIM, :].astype(BF16))


def _wres_kernel(*refs, layout, casts, epilogue):
    a_counts, w_counts, extra_counts, n_out, n_side, a_np, extra_np = layout
    pos = 0

    def take(counts):
        nonlocal pos
        groups = []
        for n in counts:
            groups.append(refs[pos:pos + n])
            pos += n
        return groups

    a_groups = take(a_counts)
    w_groups = take(w_counts)
    extra_groups = take(extra_counts)
    side_in = refs[pos:pos + n_side]
    pos += n_side
    out_refs = refs[pos:pos + n_out]
    pos += n_out
    side_out = refs[pos:pos + n_side]
    wbf_refs = refs[pos + n_side:]
    i = pl.program_id(1)

    @pl.when(i == 0)
    def _():
        for cast, w_refs, wbf_ref in zip(casts, w_groups, wbf_refs):
            cast(w_refs, wbf_ref)

    for s_in, s_out in zip(side_in, side_out):
        s_out[...] = s_in[...].astype(s_out.dtype)

    accs = [jnp.dot(_load(a_refs, n_p, i), wbf_ref[...], preferred_element_type=F32)
            for a_refs, n_p, wbf_ref in zip(a_groups, a_np, wbf_refs)]
    extras = [_load(e_refs, n_p, i) for e_refs, n_p in zip(extra_groups, extra_np)]
    for o_ref, o in zip(out_refs, epilogue(accs, extras)):
        o_ref[...] = o.astype(o_ref.dtype)


def _wres_matmul(terms, extras, outs, epilogue, *, n_col_steps, name, tm=TOKEN_TILE,
                 weight_buffers=2, side_casts=()):
    a0 = terms[0][0]
    m = (a0.prompt.shape[0] + a0.sample.shape[0]) if isinstance(a0, Stacked) else a0.shape[0]
    assert m % tm == 0
    in_arrays, in_specs, scratch, casts = [], [], [], []
    a_counts, w_counts, extra_counts, a_np, extra_np = [], [], [], [], []
    for a, w, bw, col_fns, cast in terms:
        k = w.shape[0]
        arrs, specs = _stacked_specs(a, (tm, k), lambda j, i: (i, 0), tm)
        in_arrays += arrs
        in_specs += specs
        a_counts.append(len(arrs))
        a_np.append(_n_prompt_tiles(a, tm))
    for a, w, bw, col_fns, cast in terms:
        k = w.shape[0]
        for fn in col_fns:
            in_arrays.append(w)
            in_specs.append(pl.BlockSpec((k, bw), functools.partial(lambda j, i, fn: (0, fn(j)), fn=fn),
                                         pipeline_mode=pl.Buffered(weight_buffers)))
        w_counts.append(len(col_fns))
        casts.append(cast)
        scratch.append(pltpu.VMEM((k, bw * len(col_fns)), BF16))
    for op, blk, imap in extras:
        arrs, specs = _stacked_specs(op, blk, imap, tm)
        in_arrays += arrs
        in_specs += specs
        extra_counts.append(len(arrs))
        extra_np.append(_n_prompt_tiles(op, tm))
    for arr, slab, imap in side_casts:
        in_arrays.append(arr)
        in_specs.append(pl.BlockSpec(slab, imap))
    layout = (tuple(a_counts), tuple(w_counts), tuple(extra_counts), len(outs), len(side_casts),
              tuple(a_np), tuple(extra_np))
    return pl.pallas_call(
        functools.partial(_wres_kernel, layout=layout, casts=tuple(casts), epilogue=epilogue),
        grid=(n_col_steps, m // tm),
        in_specs=in_specs,
        out_specs=([pl.BlockSpec(blk, imap) for _, blk, imap in outs]
                   + [pl.BlockSpec(slab, imap) for _, slab, imap in side_casts]),
        out_shape=([s for s, _, _ in outs]
                   + [jax.ShapeDtypeStruct(arr.shape, BF16) for arr, _, _ in side_casts]),
        scratch_shapes=scratch,
        compiler_params=_params("arbitrary", "arbitrary"),
        name=name,
    )(*in_arrays)


def _ares_kernel(a_ref, w_ref, x_ref, o_ref):
    acc = jnp.dot(a_ref[...], w_ref[...], preferred_element_type=F32)
    o_ref[...] = x_ref[...] + acc


def _ares_matmul_residual(a, w_bf, x, *, tm, tn, name):
    m, k = a.shape
    n = w_bf.shape[1]
    return pl.pallas_call(
        _ares_kernel,
        grid=(m // tm, n // tn),
        in_specs=[pl.BlockSpec((tm, k), lambda i, j: (i, 0)),
                  pl.BlockSpec((k, tn), lambda i, j: (0, j)),
                  pl.BlockSpec((tm, tn), lambda i, j: (i, j))],
        out_specs=pl.BlockSpec((tm, tn), lambda i, j: (i, j)),
        out_shape=jax.ShapeDtypeStruct((m, n), F32),
        compiler_params=_params("arbitrary", "arbitrary"),
        name=name,
    )(a, w_bf, x)


def _rope(x, cos128, sin128):
    n = x.shape[1]
    reps = n // V7X_LANES
    c = jnp.concatenate([cos128] * reps, axis=1) if reps > 1 else cos128
    s = jnp.concatenate([sin128] * reps, axis=1) if reps > 1 else sin128
    lane = lax.broadcasted_iota(jnp.int32, x.shape, 1)
    first_half = (lane & (HEAD_DIM - 1)) < (HEAD_DIM // 2)
    partner = jnp.where(first_half,
                        pltpu.roll(x, n - HEAD_DIM // 2, axis=1),
                        pltpu.roll(x, HEAD_DIM // 2, axis=1))
    return x * c + partner * s


def _epi_q(accs, extras):
    cos, sin = extras
    return [_rope(accs[0], cos, sin)]


def _epi_kv(accs, extras):
    cos, sin = extras
    acc = accs[0]
    return [_rope(acc[:, :KV_DIM], cos, sin), acc[:, KV_DIM:]]


def _epi_glu(accs, extras):
    acc = accs[0]
    half = acc.shape[1] // 2
    return [acc[:, :half] * jax.nn.sigmoid(acc[:, half:])]


def _epi_sigmoid(accs, extras):
    return [jax.nn.sigmoid(accs[0])]


def _epi_merge(accs, extras):
    g_a, g_c = extras
    return [g_a.astype(F32) * accs[0] + g_c.astype(F32) * accs[1]]


def _epi_residual(accs, extras):
    return [extras[0] + accs[0]]


def _epi_swiglu(accs, extras):
    acc = accs[0]
    half = acc.shape[1] // 2
    return [jax.nn.silu(acc[:, :half]) * acc[:, half:]]


def _attend(qs, kx, vx_ones, valid, sk, n_rep):
    r, nk = qs.shape[0], kx.shape[0]
    s = lax.dot_general(qs, kx, (((1,), (1,)), ((), ())), preferred_element_type=F32)
    s = jnp.where(valid[None], s.reshape(n_rep, r // n_rep, nk), NEG).reshape(r, nk)
    m = jnp.maximum(jnp.broadcast_to(jnp.max(s, axis=-1, keepdims=True), sk.shape), sk)
    p = jnp.exp(s - jnp.concatenate([m] * (nk // V7X_LANES), axis=1)).astype(BF16)
    ol = jnp.dot(p, vx_ones, preferred_element_type=F32)
    return ol[:, :V7X_LANES] / (ol[:, V7X_LANES:] + jnp.exp(sk - m))


def _attend_all_heads(q_ref, o_ref, sink_ref, k_of_pair, v_of_pair, valid):
    rows = q_ref.shape[0]
    nk = valid.shape[1]
    lane = lax.broadcasted_iota(jnp.int32, (nk, V7X_LANES), 1)
    ones = jnp.ones((nk, V7X_LANES), BF16)
    scale = HEAD_DIM ** -0.5
    for pair in range(N_KV_PAIRS):
        kcol = k_of_pair(pair) * scale
        vcol = v_of_pair(pair)
        c0 = pair * CHUNKS_PER_PAIR
        qs = jnp.concatenate([q_ref[:, (c0 + g) * V7X_LANES:(c0 + g + 1) * V7X_LANES]
                              for g in range(CHUNKS_PER_PAIR)], axis=0)
        o_pair = None
        for parity in range(2):
            half = (lane < HEAD_DIM) if parity == 0 else (lane >= HEAD_DIM)
            kx = jnp.where(half, kcol, 0.0).astype(BF16)
            vx = jnp.concatenate([jnp.where(half, vcol, 0.0).astype(BF16), ones], axis=1)
            o = _attend(qs, kx, vx, valid, sink_ref[2 * pair + parity], CHUNKS_PER_PAIR)
            o_pair = o if o_pair is None else o_pair + o
        for g in range(CHUNKS_PER_PAIR):
            o_ref[:, (c0 + g) * V7X_LANES:(c0 + g + 1) * V7X_LANES] = (
                o_pair[g * rows:(g + 1) * rows].astype(o_ref.dtype))


def _sink_table(sinks, rows):
    per_row = jnp.repeat(sinks.astype(F32).reshape(N_KV_HEADS, N_GROUP), rows, axis=1)
    return jnp.broadcast_to(per_row[:, :, None], (N_KV_HEADS, N_GROUP * rows, V7X_LANES))


def _attn_prompt_kernel(sink_ref, q_ref, kc_ref, kp_ref, vc_ref, vp_ref, o_ref):
    n = pl.program_id(0)
    blk = q_ref.shape[0]
    row = lax.broadcasted_iota(jnp.int32, (blk, 2 * blk), 0)
    col = lax.broadcasted_iota(jnp.int32, (blk, 2 * blk), 1)
    diff = row + blk - col
    valid = (diff >= 0) & (diff < WINDOW) & ((n > 0) | (col >= blk))

    def both_blocks(prev_ref, cur_ref):
        return lambda pair: jnp.concatenate(
            [prev_ref[:, pair * V7X_LANES:(pair + 1) * V7X_LANES],
             cur_ref[:, pair * V7X_LANES:(pair + 1) * V7X_LANES]], axis=0)

    _attend_all_heads(q_ref, o_ref, sink_ref, both_blocks(kp_ref, kc_ref), both_blocks(vp_ref, vc_ref), valid)


def _attn_prompt(q, k, v, sinks, t_prompt):
    blk = WINDOW
    nb = t_prompt // blk
    cur = lambda n: (n, 0)
    prev = lambda n: (jnp.maximum(n - 1, 0), 0)
    return pl.pallas_call(
        _attn_prompt_kernel,
        grid=(nb,),
        in_specs=[pl.BlockSpec((N_KV_HEADS, N_GROUP * blk, V7X_LANES), lambda n: (0, 0, 0)),
                  pl.BlockSpec((blk, Q_DIM), cur),
                  pl.BlockSpec((blk, KV_DIM), cur),
                  pl.BlockSpec((blk, KV_DIM), prev),
                  pl.BlockSpec((blk, KV_DIM), cur),
                  pl.BlockSpec((blk, KV_DIM), prev)],
        out_specs=pl.BlockSpec((blk, Q_DIM), cur),
        out_shape=jax.ShapeDtypeStruct((t_prompt, Q_DIM), BF16),
        compiler_params=_params("arbitrary"),
        name="attn_prompt",
    )(_sink_table(sinks, blk), q, k, k, v, v)


SAMPLE_BATCH_BLOCK = 8
NEW_KEY_PAD = 128


def _attn_sample_kernel(sink_ref, q_ref, kn_ref, vn_ref, ck_ref, cv_ref, o_ref, ko_ref, vo_ref):
    bb, n_past = ck_ref.shape[0], ck_ref.shape[1]
    rows = q_ref.shape[0]
    t_new = rows // bb
    n_cache = bb * n_past
    nk = n_cache + NEW_KEY_PAD
    lg_t, lg_p = t_new.bit_length() - 1, n_past.bit_length() - 1

    for b in range(bb):
        ko_ref[b, 0:n_past - t_new, :] = ck_ref[b, t_new:n_past, :]
        ko_ref[b, n_past - t_new:n_past, :] = kn_ref[b * t_new:(b + 1) * t_new, :]
        vo_ref[b, 0:n_past - t_new, :] = cv_ref[b, t_new:n_past, :]
        vo_ref[b, n_past - t_new:n_past, :] = vn_ref[b * t_new:(b + 1) * t_new, :]

    r = lax.broadcasted_iota(jnp.int32, (rows, nk), 0)
    c = lax.broadcasted_iota(jnp.int32, (rows, nk), 1)
    b_r, t = r >> lg_t, r & (t_new - 1)
    d_cache = t + n_past - (c & (n_past - 1))
    valid_cache = (c < n_cache) & ((c >> lg_p) == b_r) & (d_cache >= 0) & (d_cache < WINDOW)
    cn = c - n_cache
    d_new = t - (cn & (t_new - 1))
    valid_new = (cn >= 0) & (cn < rows) & ((cn >> lg_t) == b_r) & (d_new >= 0) & (d_new < WINDOW)
    valid = valid_cache | valid_new

    pad = jnp.zeros((NEW_KEY_PAD - rows, V7X_LANES), F32)

    def keys(cache_ref, new_ref):
        def of_pair(pair):
            ls = slice(pair * V7X_LANES, (pair + 1) * V7X_LANES)
            return jnp.concatenate([cache_ref[:, :, ls].reshape(n_cache, V7X_LANES), new_ref[:, ls], pad], axis=0)
        return of_pair

    _attend_all_heads(q_ref, o_ref, sink_ref, keys(ck_ref, kn_ref), keys(cv_ref, vn_ref), valid)


def _attn_sample(q, k, v, cache_k, cache_v, sinks, row0, t_new):
    b, n_past, _ = cache_k.shape
    assert n_past == WINDOW and t_new & (t_new - 1) == 0
    bb = SAMPLE_BATCH_BLOCK
    rows = bb * t_new
    assert row0 % rows == 0 and rows <= NEW_KEY_PAD
    tok = lambda i: (row0 // rows + i, 0)
    i3 = lambda i: (i, 0, 0)
    return pl.pallas_call(
        _attn_sample_kernel,
        grid=(b // bb,),
        in_specs=[pl.BlockSpec((N_KV_HEADS, N_GROUP * rows, V7X_LANES), lambda i: (0, 0, 0)),
                  pl.BlockSpec((rows, Q_DIM), tok),
                  pl.BlockSpec((rows, KV_DIM), tok),
                  pl.BlockSpec((rows, KV_DIM), tok),
                  pl.BlockSpec((bb, n_past, KV_DIM), i3),
                  pl.BlockSpec((bb, n_past, KV_DIM), i3)],
        out_specs=[pl.BlockSpec((rows, Q_DIM), lambda i: (i, 0)),
                   pl.BlockSpec((bb, n_past, KV_DIM), i3),
                   pl.BlockSpec((bb, n_past, KV_DIM), i3)],
        out_shape=[jax.ShapeDtypeStruct((b * t_new, Q_DIM), BF16),
                   jax.ShapeDtypeStruct((b, n_past, KV_DIM), F32),
                   jax.ShapeDtypeStruct((b, n_past, KV_DIM), F32)],
        compiler_params=_params("arbitrary"),
        name="attn_sample",
    )(_sink_table(sinks, rows), q, k, v, cache_k, cache_v)


def _ln_swish(c, g, b):
    mu = jnp.mean(c, axis=-1, keepdims=True)
    var = jnp.mean(jnp.square(c - mu), axis=-1, keepdims=True)
    y = (c - mu) * lax.rsqrt(var + EPS) * g + b
    return jax.nn.silu(y)


CONV_HALO = 32
SUBLANES = 8
CONV_ROW_TILE = 128


def _conv_prompt_kernel(cur_ref, halo_ref, w_ref, b_ref, g_ref, bl_ref, o_ref, xs_ref, c_ref):
    i = pl.program_id(0)
    tt = cur_ref.shape[0]
    xs_ref[0:CONV_HALO, :] = jnp.where(i > 0, halo_ref[...], 0.0)
    xs_ref[CONV_HALO:CONV_HALO + tt, :] = cur_ref[...]
    off = CONV_HALO - (CONV_WIDTH - 1)
    n_groups = CONV_ROW_TILE // SUBLANES
    taps_by_shift = [[(w, (off + w) // SUBLANES) for w in range(CONV_WIDTH) if (off + w) % SUBLANES == s]
                     for s in range(SUBLANES)]
    n_in = max(n_groups + max(d for _, d in taps) + (s > 0) for s, taps in enumerate(taps_by_shift))
    assert tt - CONV_ROW_TILE + n_in * SUBLANES <= CONV_HALO + tt
    sub = lax.broadcasted_iota(jnp.int32, (SUBLANES, V7X_LANES), 0)

    def lane_chunk(cc, carry):
        ls = pl.ds(pl.multiple_of(cc * V7X_LANES, V7X_LANES), V7X_LANES)
        for r0 in range(0, tt, CONV_ROW_TILE):
            x0 = [xs_ref[pl.ds(r0 + kg * SUBLANES, SUBLANES), ls] for kg in range(n_in)]
            acc = [None] * n_groups
            for s, taps in enumerate(taps_by_shift):
                wv = {w: jnp.broadcast_to(w_ref[w:w + 1, ls], (SUBLANES, V7X_LANES)) for w, _ in taps}
                n_used = n_groups + max(d for _, d in taps)
                if s == 0:
                    xsft = x0
                else:
                    rot = [pltpu.roll(x0[kg], SUBLANES - s, axis=0) for kg in range(n_used + 1)]
                    xsft = [jnp.where(sub < SUBLANES - s, rot[kg], rot[kg + 1]) for kg in range(n_used)]
                for kg in range(n_used):
                    for w, d in taps:
                        og = kg - d
                        if 0 <= og < n_groups:
                            term = xsft[kg] * wv[w]
                            acc[og] = term if acc[og] is None else acc[og] + term
            bias = b_ref[:, ls]
            for og in range(n_groups):
                c_ref[pl.ds(r0 + og * SUBLANES, SUBLANES), ls] = acc[og] + bias
        return carry

    lax.fori_loop(0, cur_ref.shape[1] // V7X_LANES, lane_chunk, 0)
    o_ref[...] = _ln_swish(c_ref[...], g_ref[...], bl_ref[...]).astype(o_ref.dtype)


def _conv_prompt(cu, w_dw, b_dw, g_ln, b_ln, t_prompt):
    tt = 256
    c = cu.shape[1]
    per = tt // CONV_HALO
    row1 = lambda i: (0, 0)
    return pl.pallas_call(
        _conv_prompt_kernel,
        grid=(t_prompt // tt,),
        in_specs=[pl.BlockSpec((tt, c), lambda i: (i, 0)),
                  pl.BlockSpec((CONV_HALO, c), lambda i: (jnp.maximum(i * per - 1, 0), 0)),
                  pl.BlockSpec((CONV_WIDTH, c), row1),
                  pl.BlockSpec((1, c), row1),
                  pl.BlockSpec((1, c), row1),
                  pl.BlockSpec((1, c), row1)],
        out_specs=pl.BlockSpec((tt, c), lambda i: (i, 0)),
        out_shape=jax.ShapeDtypeStruct((t_prompt, c), BF16),
        scratch_shapes=[pltpu.VMEM((CONV_HALO + tt, c), F32), pltpu.VMEM((tt, c), F32)],
        compiler_params=_params("arbitrary"),
        name="conv_prompt",
    )(cu, cu, w_dw, b_dw.reshape(1, c), g_ln.reshape(1, c), b_ln.reshape(1, c))


def _conv_sample_kernel(st_ref, cu_ref, w_ref, b_ref, g_ref, bl_ref, o_ref, so_ref, xs_ref):
    bb, n_hist, c = st_ref.shape
    t_new = cu_ref.shape[0] // bb
    for b in range(bb):
        xs_ref[0:n_hist, :] = st_ref[b]
        xs_ref[n_hist:n_hist + t_new, :] = cu_ref[b * t_new:(b + 1) * t_new, :]
        so_ref[b] = xs_ref[t_new:t_new + n_hist, :]
        acc = xs_ref[0:t_new, :] * w_ref[0:1, :]
        for w in range(1, CONV_WIDTH):
            acc = acc + xs_ref[w:w + t_new, :] * w_ref[w:w + 1, :]
        o_ref[b * t_new:(b + 1) * t_new, :] = _ln_swish(
            acc + b_ref[...], g_ref[...], bl_ref[...]).astype(o_ref.dtype)


def _conv_sample(cu, state, w_dw, b_dw, g_ln, b_ln, row0, t_new):
    b, n_hist, c = state.shape
    bb = SAMPLE_BATCH_BLOCK
    rows = bb * t_new
    assert row0 % rows == 0
    i3 = lambda i: (i, 0, 0)
    row1 = lambda i: (0, 0)
    return pl.pallas_call(
        _conv_sample_kernel,
        grid=(b // bb,),
        in_specs=[pl.BlockSpec((bb, n_hist, c), i3),
                  pl.BlockSpec((rows, c), lambda i: (row0 // rows + i, 0)),
                  pl.BlockSpec((CONV_WIDTH, c), row1),
                  pl.BlockSpec((1, c), row1),
                  pl.BlockSpec((1, c), row1),
                  pl.BlockSpec((1, c), row1)],
        out_specs=[pl.BlockSpec((rows, c), lambda i: (i, 0)),
                   pl.BlockSpec((bb, n_hist, c), i3)],
        out_shape=[jax.ShapeDtypeStruct((b * t_new, c), BF16),
                   jax.ShapeDtypeStruct((b, n_hist, c), F32)],
        scratch_shapes=[pltpu.VMEM((n_hist + t_new + 6, c), F32)],
        compiler_params=_params("arbitrary"),
        name="conv_sample",
    )(state, cu, w_dw, b_dw.reshape(1, c), g_ln.reshape(1, c), b_ln.reshape(1, c))


def _rope_tables(pos):
    half = HEAD_DIM // 2
    inv_freq = ROPE_THETA ** (-jnp.arange(half, dtype=F32) / half)
    ang = pos.astype(F32)[:, None] * inv_freq[None, :]
    cos, sin = jnp.cos(ang), jnp.sin(ang)
    cos64 = jnp.concatenate([cos, cos], axis=-1)
    sin64 = jnp.concatenate([-sin, sin], axis=-1)
    reps = V7X_LANES // HEAD_DIM
    return jnp.tile(cos64, (1, reps)), jnp.tile(sin64, (1, reps))


FFN_TOKEN_TILE = 1088
FFN_BLOCK = 256


def kernel(x_prompt, x_sample, cache_k, cache_v, state_conv, g_mix_norm, w_in, sinks, w_attn_o,
           w_dw, b_dw, g_conv_ln, b_conv_ln, w_conv_o, w_out, g_ffn_norm, w_ffn_in, w_ffn_out,
           g_final):
    depth = w_in.shape[0]
    assert depth == 1, "single-layer trunk"
    t_p = x_prompt.shape[1]
    b_s, t_s, _ = x_sample.shape
    m = t_p + b_s * t_s
    tm = TOKEN_TILE
    bw = WEIGHT_BLOCK
    sds = jax.ShapeDtypeStruct
    plain = _cast_plain

    x = Stacked(x_prompt.reshape(t_p, D_MODEL), x_sample.reshape(b_s * t_s, D_MODEL))
    pos = jnp.concatenate([jnp.arange(t_p, dtype=jnp.int32),
                           jnp.tile(PAST_LEN + jnp.arange(t_s, dtype=jnp.int32), b_s)])
    cos_t, sin_t = _rope_tables(pos)
    rope_extras = [(cos_t, (tm, V7X_LANES), lambda j, i: (i, 0)),
                   (sin_t, (tm, V7X_LANES), lambda j, i: (i, 0))]

    l = 0
    w_in_l = w_in[l]
    h = _rmsnorm(x, g_mix_norm[l], BF16, "rmsnorm_mix")

    kv_blk0 = Q_DIM // bw
    ua_blk0 = (Q_DIM + 2 * KV_DIM) // bw
    ub_blk0 = ua_blk0 + CONV_DIM // bw
    ga_blk0 = ub_blk0 + CONV_DIM // bw
    assert 2 * KV_DIM == bw and N_GROUP * HEAD_DIM == bw

    (q,) = _wres_matmul(
        [(h, w_in_l, bw, [lambda j: 2 * j, lambda j: 2 * j + 1], _cast_interleave_heads)],
        rope_extras,
        [(sds((m, Q_DIM), BF16), (tm, 2 * bw), lambda j, i: (i, j))],
        _epi_q, n_col_steps=N_KV_PAIRS, name="proj_q")

    k_all, v_all = _wres_matmul(
        [(h, w_in_l, bw, [lambda j: kv_blk0], plain)],
        rope_extras,
        [(sds((m, KV_DIM), F32), (tm, KV_DIM), lambda j, i: (i, 0)),
         (sds((m, KV_DIM), F32), (tm, KV_DIM), lambda j, i: (i, 0))],
        _epi_kv, n_col_steps=1, name="proj_kv")

    (cu,) = _wres_matmul(
        [(h, w_in_l, bw, [lambda j: ua_blk0 + j, lambda j: ub_blk0 + j], plain)],
        [],
        [(sds((m, CONV_DIM), F32), (tm, bw), lambda j, i: (i, j))],
        _epi_glu, n_col_steps=CONV_DIM // bw, name="proj_glu")

    (gates,) = _wres_matmul(
        [(h, w_in_l, bw, [lambda j: ga_blk0 + 2 * j, lambda j: ga_blk0 + 2 * j + 1], plain)],
        [],
        [(sds((m, 2 * D_MODEL), BF16), (tm, 2 * bw), lambda j, i: (i, j))],
        _epi_sigmoid, n_col_steps=2 * D_MODEL // (2 * bw), name="proj_gates")

    n_past = cache_k.shape[2]
    o_p = _attn_prompt(q, k_all, v_all, sinks[l], t_p)
    o_s, k_win, v_win = _attn_sample(q, k_all, v_all,
                                     cache_k[l].reshape(b_s, n_past, KV_DIM),
                                     cache_v[l].reshape(b_s, n_past, KV_DIM), sinks[l], t_p, t_s)

    c_p = _conv_prompt(cu, w_dw[l], b_dw[l], g_conv_ln[l], b_conv_ln[l], t_p)
    c_s, conv_state = _conv_sample(cu, state_conv[l], w_dw[l], b_dw[l], g_conv_ln[l], b_conv_ln[l],
                                   t_p, t_s)

    n_gate_blk = D_MODEL // (2 * bw)
    two_blocks = [lambda j: 2 * j, lambda j: 2 * j + 1]
    (mix,) = _wres_matmul(
        [(Stacked(o_p, o_s), w_attn_o[l], bw, two_blocks, _cast_permute_head_rows),
         (Stacked(c_p, c_s), w_conv_o[l], bw, two_blocks, plain)],
        [(gates, (tm, 2 * bw), lambda j, i: (i, j)),
         (gates, (tm, 2 * bw), lambda j, i: (i, j + n_gate_blk))],
        [(sds((m, D_MODEL), BF16), (tm, 2 * bw), lambda j, i: (i, j))],
        _epi_merge, n_col_steps=n_gate_blk, name="merge", weight_buffers=1)

    (x1,) = _wres_matmul(
        [(mix, w_out[l], bw, two_blocks, plain)],
        [(x, (tm, 2 * bw), lambda j, i: (i, j))],
        [(sds((m, D_MODEL), F32), (tm, 2 * bw), lambda j, i: (i, j))],
        _epi_residual, n_col_steps=D_MODEL // (2 * bw), name="out_proj", weight_buffers=1)

    h2 = _rmsnorm(x1, g_ffn_norm[l], BF16, "rmsnorm_ffn")
    ftm = FFN_TOKEN_TILE
    n_ff_blk = D_FF // FFN_BLOCK
    n_ff_steps = n_ff_blk * (m // ftm)
    slab = D_FF // n_ff_steps
    assert m % ftm == 0 and slab * n_ff_steps == D_FF and slab % 16 == 0
    slab_map = lambda j, i: (j * (m // ftm) + i, 0)
    act, w_ffn_out_bf = _wres_matmul(
        [(h2, w_ffn_in[l], FFN_BLOCK, [lambda j: j, lambda j: j + n_ff_blk], plain)],
        [],
        [(sds((m, D_FF), BF16), (ftm, FFN_BLOCK), lambda j, i: (i, j))],
        _epi_swiglu, n_col_steps=n_ff_blk, name="ffn_in", tm=ftm,
        side_casts=[(w_ffn_out[l], (slab, D_MODEL), slab_map)])

    x2 = _ares_matmul_residual(act, w_ffn_out_bf, x1, tm=tm, tn=512, name="ffn_out")
    norm_tile = 256
    y_p = _rmsnorm(x2, g_final, F32, "rmsnorm_final_prompt", row_tile0=0, n_tiles=t_p // norm_tile,
                   tm=norm_tile)
    y_s = _rmsnorm(x2, g_final, F32, "rmsnorm_final_sample", row_tile0=t_p // norm_tile,
                   n_tiles=(m - t_p) // norm_tile, tm=norm_tile)

    y_prompt = y_p.reshape(x_prompt.shape)
    y_sample = y_s.reshape(x_sample.shape)
    k_prompt = k_all[t_p - WINDOW:t_p].reshape(1, 1, WINDOW, N_KV_HEADS, HEAD_DIM)
    v_prompt = v_all[t_p - WINDOW:t_p].reshape(1, 1, WINDOW, N_KV_HEADS, HEAD_DIM)
    conv_prompt = cu[t_p - (CONV_WIDTH - 1):t_p].reshape(1, 1, CONV_WIDTH - 1, CONV_DIM)
    k_sample = k_win.reshape(1, b_s, WINDOW, N_KV_HEADS, HEAD_DIM)
    v_sample = v_win.reshape(1, b_s, WINDOW, N_KV_HEADS, HEAD_DIM)
    conv_sample = conv_state.reshape(1, b_s, CONV_WIDTH - 1, CONV_DIM)
    return (y_prompt, y_sample, k_prompt, v_prompt, conv_prompt, k_sample, v_sample, conv_sample)
```

```python
import collections
import functools

import jax
import jax.numpy as jnp
from jax import lax
from jax.experimental import pallas as pl
from jax.experimental.pallas import tpu as pltpu

D_MODEL = 4096
HEAD_DIM = 64
N_HEADS = 32
N_KV_HEADS = 4
N_GROUP = N_HEADS // N_KV_HEADS
WINDOW = 128
ROPE_THETA = 10000.0
CONV_DIM = 2048
CONV_WIDTH = 31
D_FF = 11008
PAST_LEN = 8192
Q_DIM = N_HEADS * HEAD_DIM
KV_DIM = N_KV_HEADS * HEAD_DIM
EPS = 1e-6
NEG = -1e30

V7X_LANES = 128
V7X_VMEM_LIMIT_BYTES = 60000 * 1024

F32 = jnp.float32
BF16 = jnp.bfloat16

TOKEN_TILE = 512
WEIGHT_BLOCK = 512
CAST_ROWS = 256
N_KV_PAIRS = N_KV_HEADS // 2
CHUNKS_PER_PAIR = N_GROUP

Stacked = collections.namedtuple("Stacked", ["prompt", "sample"])

SideCast = collections.namedtuple("SideCast", ["array", "slab", "in_map", "out_shape", "out_map"])


def _slab_stream(array, slab, n_outer, n_inner, *, col0=0, n_cols=None):
    k, n = array.shape
    n_cols = n - col0 if n_cols is None else n_cols
    assert k % slab[0] == 0 and n_cols % slab[1] == 0 and col0 % slab[1] == 0
    n_r, n_c = k // slab[0], n_cols // slab[1]
    assert n_r * n_c <= n_outer * n_inner, "not enough grid steps to cast every slab"

    def rc(j, i):
        t = jnp.minimum(j * n_inner + i, n_r * n_c - 1)
        return t // n_c, t % n_c

    def in_map(j, i):
        r, c = rc(j, i)
        return (r, col0 // slab[1] + c)

    return SideCast(array, slab, in_map, (k, n_cols), rc)


def _params(*sem):
    return pltpu.CompilerParams(dimension_semantics=sem,
                                vmem_limit_bytes=V7X_VMEM_LIMIT_BYTES)


def _stacked_specs(op, block, imap, tm):
    if not isinstance(op, Stacked):
        return [op], [pl.BlockSpec(block, imap)]
    n_p = op.prompt.shape[0] // tm
    assert op.prompt.shape[0] % tm == 0 and op.sample.shape[0] % tm == 0

    def p_map(*ids):
        r, c = imap(*ids)
        return (jnp.minimum(r, n_p - 1), c)

    def s_map(*ids):
        r, c = imap(*ids)
        return (jnp.maximum(r - n_p, 0), c)

    return [op.prompt, op.sample], [pl.BlockSpec(block, p_map), pl.BlockSpec(block, s_map)]


def _n_prompt_tiles(op, tm):
    return op.prompt.shape[0] // tm if isinstance(op, Stacked) else None


def _load(refs, n_p, row_tile):
    if n_p is None:
        return refs[0][...]
    return jnp.where(row_tile < n_p, refs[0][...], refs[1][...])


def _rmsnorm_kernel(*refs, n_p):
    x_refs, (g_ref, o_ref) = refs[:-2], refs[-2:]
    x = _load(x_refs, n_p, pl.program_id(0))
    y = x * lax.rsqrt(jnp.mean(x * x, axis=-1, keepdims=True) + EPS) * g_ref[...]
    o_ref[...] = y.astype(o_ref.dtype)


def _rmsnorm(x, g, out_dtype, name, *, row_tile0=0, n_tiles=None, tm=256):
    d = g.shape[0]
    rows = (x.prompt.shape[0] + x.sample.shape[0]) if isinstance(x, Stacked) else x.shape[0]
    n_tiles = rows // tm if n_tiles is None else n_tiles
    arrays, specs = _stacked_specs(x, (tm, d), lambda i: (i + row_tile0, 0), tm)
    return pl.pallas_call(
        functools.partial(_rmsnorm_kernel, n_p=_n_prompt_tiles(x, tm)),
        grid=(n_tiles,),
        in_specs=specs + [pl.BlockSpec((1, d), lambda i: (0, 0))],
        out_specs=pl.BlockSpec((tm, d), lambda i: (i, 0)),
        out_shape=jax.ShapeDtypeStruct((n_tiles * tm, d), out_dtype),
        compiler_params=_params("arbitrary"),
        name=name,
    )(*arrays, g.reshape(1, d))


def _cast_plain(w_refs, wbf_ref):
    k, bw = w_refs[0].shape

    def body(c, carry):
        r = pl.multiple_of(c * CAST_ROWS, CAST_ROWS)
        for b, w_ref in enumerate(w_refs):
            wbf_ref[pl.ds(r, CAST_ROWS), b * bw:(b + 1) * bw] = w_ref[pl.ds(r, CAST_ROWS), :].astype(BF16)
        return carry

    lax.fori_loop(0, k // CAST_ROWS, body, 0)


def _cast_interleave_heads(w_refs, wbf_ref):
    w_even, w_odd = w_refs
    k, bw = w_even.shape

    def body(c, carry):
        r = pl.multiple_of(c * CAST_ROWS, CAST_ROWS)
        a = w_even[pl.ds(r, CAST_ROWS), :]
        b = w_odd[pl.ds(r, CAST_ROWS), :]
        pieces = []
        for g in range(bw // HEAD_DIM):
            pieces += [a[:, g * HEAD_DIM:(g + 1) * HEAD_DIM], b[:, g * HEAD_DIM:(g + 1) * HEAD_DIM]]
        wbf_ref[pl.ds(r, CAST_ROWS), :] = jnp.concatenate(pieces, axis=1).astype(BF16)
        return carry

    lax.fori_loop(0, k // CAST_ROWS, body, 0)


def _interleaved_head(slot):
    chunk, parity = divmod(slot, 2)
    pair, g = divmod(chunk, CHUNKS_PER_PAIR)
    return (2 * pair + parity) * N_GROUP + g


def _cast_permute_head_rows(w_refs, wbf_ref):
    bw = w_refs[0].shape[1]
    for b, w_ref in enumerate(w_refs):
        for slot in range(N_HEADS):
            h = _interleaved_head(slot)
            wbf_ref[slot * HEAD_DIM:(slot + 1) * HEAD_DIM, b * bw:(b + 1) * bw] = (
                w_ref[h * HEAD_DIM:(h + 1) * HEAD_DIM, :].astype(BF16))


def _wres_kernel(*refs, layout, casts, epilogue):
    a_counts, w_counts, extra_counts, n_out, n_side, a_np, extra_np = layout
    pos = 0

    def take(counts):
        nonlocal pos
        groups = []
        for n in counts:
            groups.append(refs[pos:pos + n])
            pos += n
        return groups

    a_groups = take(a_counts)
    w_groups = take(w_counts)
    extra_groups = take(extra_counts)
    side_in = refs[pos:pos + n_side]
    pos += n_side
    out_refs = refs[pos:pos + n_out]
    pos += n_out
    side_out = refs[pos:pos + n_side]
    scratch_refs = list(refs[pos + n_side:])
    wbf_refs = [scratch_refs.pop(0) if cast is not None else None for cast in casts]
    i = pl.program_id(1)

    if any(cast is not None for cast in casts):
        @pl.when(i == 0)
        def _():
            for cast, w_refs, wbf_ref in zip(casts, w_groups, wbf_refs):
                if cast is not None:
                    cast(w_refs, wbf_ref)

    for s_in, s_out in zip(side_in, side_out):
        s_out[...] = s_in[...].astype(s_out.dtype)

    accs = []
    for a_refs, n_p, w_refs, wbf_ref in zip(a_groups, a_np, w_groups, wbf_refs):
        a = _load(a_refs, n_p, i)
        if wbf_ref is not None:
            accs.append(jnp.dot(a, wbf_ref[...], preferred_element_type=F32))
        else:
            parts = [jnp.dot(a, w_ref[...], preferred_element_type=F32) for w_ref in w_refs]
            accs.append(parts[0] if len(parts) == 1 else jnp.concatenate(parts, axis=1))
    extras = [_load(e_refs, n_p, i) for e_refs, n_p in zip(extra_groups, extra_np)]
    for o_ref, o in zip(out_refs, epilogue(accs, extras)):
        o_ref[...] = o.astype(o_ref.dtype)


def _wres_matmul(terms, extras, outs, epilogue, *, n_col_steps, name, tm=TOKEN_TILE,
                 weight_buffers=2, side_casts=()):
    a0 = terms[0][0]
    m = (a0.prompt.shape[0] + a0.sample.shape[0]) if isinstance(a0, Stacked) else a0.shape[0]
    assert m % tm == 0
    in_arrays, in_specs, scratch, casts = [], [], [], []
    a_counts, w_counts, extra_counts, a_np, extra_np = [], [], [], [], []
    for a, w, bw, col_fns, cast in terms:
        k = w.shape[0]
        arrs, specs = _stacked_specs(a, (tm, k), lambda j, i: (i, 0), tm)
        in_arrays += arrs
        in_specs += specs
        a_counts.append(len(arrs))
        a_np.append(_n_prompt_tiles(a, tm))
    for a, w, bw, col_fns, cast in terms:
        k = w.shape[0]
        for fn in col_fns:
            in_arrays.append(w)
            in_specs.append(pl.BlockSpec((k, bw), functools.partial(lambda j, i, fn: (0, fn(j)), fn=fn),
                                         pipeline_mode=pl.Buffered(weight_buffers)))
        w_counts.append(len(col_fns))
        casts.append(cast)
        if cast is not None:
            scratch.append(pltpu.VMEM((k, bw * len(col_fns)), BF16))
    for op, blk, imap in extras:
        arrs, specs = _stacked_specs(op, blk, imap, tm)
        in_arrays += arrs
        in_specs += specs
        extra_counts.append(len(arrs))
        extra_np.append(_n_prompt_tiles(op, tm))
    for sc in side_casts:
        in_arrays.append(sc.array)
        in_specs.append(pl.BlockSpec(sc.slab, sc.in_map))
    layout = (tuple(a_counts), tuple(w_counts), tuple(extra_counts), len(outs), len(side_casts),
              tuple(a_np), tuple(extra_np))
    return pl.pallas_call(
        functools.partial(_wres_kernel, layout=layout, casts=tuple(casts), epilogue=epilogue),
        grid=(n_col_steps, m // tm),
        in_specs=in_specs,
        out_specs=([pl.BlockSpec(blk, imap) for _, blk, imap in outs]
                   + [pl.BlockSpec(sc.slab, sc.out_map) for sc in side_casts]),
        out_shape=([s for s, _, _ in outs]
                   + [jax.ShapeDtypeStruct(sc.out_shape, BF16) for sc in side_casts]),
        scratch_shapes=scratch,
        compiler_params=_params("arbitrary", "arbitrary"),
        name=name,
    )(*in_arrays)


def _ares_kernel(a_ref, w_ref, x_ref, o_ref):
    acc = jnp.dot(a_ref[...], w_ref[...], preferred_element_type=F32)
    o_ref[...] = x_ref[...] + acc


def _ares_matmul_residual(a, w_bf, x, *, tm, tn, name):
    m, k = a.shape
    n = w_bf.shape[1]
    return pl.pallas_call(
        _ares_kernel,
        grid=(m // tm, n // tn),
        in_specs=[pl.BlockSpec((tm, k), lambda i, j: (i, 0)),
                  pl.BlockSpec((k, tn), lambda i, j: (0, j)),
                  pl.BlockSpec((tm, tn), lambda i, j: (i, j))],
        out_specs=pl.BlockSpec((tm, tn), lambda i, j: (i, j)),
        out_shape=jax.ShapeDtypeStruct((m, n), F32),
        compiler_params=_params("arbitrary", "arbitrary"),
        name=name,
    )(a, w_bf, x)


def _rope(x, cos128, sin128):
    n = x.shape[1]
    reps = n // V7X_LANES
    c = jnp.concatenate([cos128] * reps, axis=1) if reps > 1 else cos128
    s = jnp.concatenate([sin128] * reps, axis=1) if reps > 1 else sin128
    lane = lax.broadcasted_iota(jnp.int32, x.shape, 1)
    first_half = (lane & (HEAD_DIM - 1)) < (HEAD_DIM // 2)
    partner = jnp.where(first_half,
                        pltpu.roll(x, n - HEAD_DIM // 2, axis=1),
                        pltpu.roll(x, HEAD_DIM // 2, axis=1))
    return x * c + partner * s


def _epi_q(accs, extras):
    cos, sin = extras
    return [_rope(accs[0], cos, sin)]


def _epi_kv(accs, extras):
    cos, sin = extras
    acc = accs[0]
    return [_rope(acc[:, :KV_DIM], cos, sin), acc[:, KV_DIM:]]


def _epi_glu(accs, extras):
    acc = accs[0]
    half = acc.shape[1] // 2
    return [acc[:, :half] * jax.nn.sigmoid(acc[:, half:])]


def _epi_sigmoid(accs, extras):
    return [jax.nn.sigmoid(accs[0])]


def _epi_merge(accs, extras):
    g_a, g_c = extras
    return [g_a.astype(F32) * accs[0] + g_c.astype(F32) * accs[1]]


def _epi_residual(accs, extras):
    return [extras[0] + accs[0]]


def _epi_swiglu(accs, extras):
    acc = accs[0]
    half = acc.shape[1] // 2
    return [jax.nn.silu(acc[:, :half]) * acc[:, half:]]


def _attend(qs, kx, vx_ones, valid, sk, n_rep):
    r, nk = qs.shape[0], kx.shape[0]
    s = lax.dot_general(qs, kx, (((1,), (1,)), ((), ())), preferred_element_type=F32)
    s = jnp.where(valid[None], s.reshape(n_rep, r // n_rep, nk), NEG).reshape(r, nk)
    m = jnp.maximum(jnp.broadcast_to(jnp.max(s, axis=-1, keepdims=True), sk.shape), sk)
    p = jnp.exp(s - jnp.concatenate([m] * (nk // V7X_LANES), axis=1)).astype(BF16)
    ol = jnp.dot(p, vx_ones, preferred_element_type=F32)
    return ol[:, :V7X_LANES] / (ol[:, V7X_LANES:] + jnp.exp(sk - m))


def _attend_all_heads(q_ref, o_ref, sink_ref, k_of_pair, v_of_pair, valid):
    rows = q_ref.shape[0]
    nk = valid.shape[1]
    lane = lax.broadcasted_iota(jnp.int32, (nk, V7X_LANES), 1)
    ones = jnp.ones((nk, V7X_LANES), BF16)
    scale = HEAD_DIM ** -0.5
    for pair in range(N_KV_PAIRS):
        kcol = k_of_pair(pair) * scale
        vcol = v_of_pair(pair)
        c0 = pair * CHUNKS_PER_PAIR
        qs = jnp.concatenate([q_ref[:, (c0 + g) * V7X_LANES:(c0 + g + 1) * V7X_LANES]
                              for g in range(CHUNKS_PER_PAIR)], axis=0)
        o_pair = None
        for parity in range(2):
            half = (lane < HEAD_DIM) if parity == 0 else (lane >= HEAD_DIM)
            kx = jnp.where(half, kcol, 0.0).astype(BF16)
            vx = jnp.concatenate([jnp.where(half, vcol, 0.0).astype(BF16), ones], axis=1)
            o = _attend(qs, kx, vx, valid, sink_ref[2 * pair + parity], CHUNKS_PER_PAIR)
            o_pair = o if o_pair is None else o_pair + o
        for g in range(CHUNKS_PER_PAIR):
            o_ref[:, (c0 + g) * V7X_LANES:(c0 + g + 1) * V7X_LANES] = (
                o_pair[g * rows:(g + 1) * rows].astype(o_ref.dtype))


def _sink_table(sinks, rows):
    per_row = jnp.repeat(sinks.astype(F32).reshape(N_KV_HEADS, N_GROUP), rows, axis=1)
    return jnp.broadcast_to(per_row[:, :, None], (N_KV_HEADS, N_GROUP * rows, V7X_LANES))


def _attn_prompt_kernel(sink_ref, q_ref, kc_ref, kp_ref, vc_ref, vp_ref, side_ref, o_ref, side_o_ref):
    n = pl.program_id(0)
    side_o_ref[...] = side_ref[...].astype(side_o_ref.dtype)
    blk = q_ref.shape[0]
    row = lax.broadcasted_iota(jnp.int32, (blk, 2 * blk), 0)
    col = lax.broadcasted_iota(jnp.int32, (blk, 2 * blk), 1)
    diff = row + blk - col
    valid = (diff >= 0) & (diff < WINDOW) & ((n > 0) | (col >= blk))

    def both_blocks(prev_ref, cur_ref):
        return lambda pair: jnp.concatenate(
            [prev_ref[:, pair * V7X_LANES:(pair + 1) * V7X_LANES],
             cur_ref[:, pair * V7X_LANES:(pair + 1) * V7X_LANES]], axis=0)

    _attend_all_heads(q_ref, o_ref, sink_ref, both_blocks(kp_ref, kc_ref), both_blocks(vp_ref, vc_ref), valid)


def _attn_prompt(q, k, v, sinks, t_prompt, make_side_cast):
    blk = WINDOW
    nb = t_prompt // blk
    side = make_side_cast(nb)
    cur = lambda n: (n, 0)
    prev = lambda n: (jnp.maximum(n - 1, 0), 0)
    return pl.pallas_call(
        _attn_prompt_kernel,
        grid=(nb,),
        in_specs=[pl.BlockSpec((N_KV_HEADS, N_GROUP * blk, V7X_LANES), lambda n: (0, 0, 0)),
                  pl.BlockSpec((blk, Q_DIM), cur),
                  pl.BlockSpec((blk, KV_DIM), cur),
                  pl.BlockSpec((blk, KV_DIM), prev),
                  pl.BlockSpec((blk, KV_DIM), cur),
                  pl.BlockSpec((blk, KV_DIM), prev),
                  pl.BlockSpec(side.slab, lambda n: side.in_map(0, n))],
        out_specs=[pl.BlockSpec((blk, Q_DIM), cur),
                   pl.BlockSpec(side.slab, lambda n: side.out_map(0, n))],
        out_shape=[jax.ShapeDtypeStruct((t_prompt, Q_DIM), BF16),
                   jax.ShapeDtypeStruct(side.out_shape, BF16)],
        compiler_params=_params("arbitrary"),
        name="attn_prompt",
    )(_sink_table(sinks, blk), q, k, k, v, v, side.array)


SAMPLE_BATCH_BLOCK = 8
NEW_KEY_PAD = 128


def _attn_sample_kernel(sink_ref, q_ref, kn_ref, vn_ref, ck_ref, cv_ref, o_ref, ko_ref, vo_ref):
    bb, n_past = ck_ref.shape[0], ck_ref.shape[1]
    rows = q_ref.shape[0]
    t_new = rows // bb
    n_cache = bb * n_past
    nk = n_cache + NEW_KEY_PAD
    lg_t, lg_p = t_new.bit_length() - 1, n_past.bit_length() - 1

    for b in range(bb):
        ko_ref[b, 0:n_past - t_new, :] = ck_ref[b, t_new:n_past, :]
        ko_ref[b, n_past - t_new:n_past, :] = kn_ref[b * t_new:(b + 1) * t_new, :]
        vo_ref[b, 0:n_past - t_new, :] = cv_ref[b, t_new:n_past, :]
        vo_ref[b, n_past - t_new:n_past, :] = vn_ref[b * t_new:(b + 1) * t_new, :]

    r = lax.broadcasted_iota(jnp.int32, (rows, nk), 0)
    c = lax.broadcasted_iota(jnp.int32, (rows, nk), 1)
    b_r, t = r >> lg_t, r & (t_new - 1)
    d_cache = t + n_past - (c & (n_past - 1))
    valid_cache = (c < n_cache) & ((c >> lg_p) == b_r) & (d_cache >= 0) & (d_cache < WINDOW)
    cn = c - n_cache
    d_new = t - (cn & (t_new - 1))
    valid_new = (cn >= 0) & (cn < rows) & ((cn >> lg_t) == b_r) & (d_new >= 0) & (d_new < WINDOW)
    valid = valid_cache | valid_new

    pad = jnp.zeros((NEW_KEY_PAD - rows, V7X_LANES), F32)

    def keys(cache_ref, new_ref):
        def of_pair(pair):
            ls = slice(pair * V7X_LANES, (pair + 1) * V7X_LANES)
            return jnp.concatenate([cache_ref[:, :, ls].reshape(n_cache, V7X_LANES), new_ref[:, ls], pad], axis=0)
        return of_pair

    _attend_all_heads(q_ref, o_ref, sink_ref, keys(ck_ref, kn_ref), keys(cv_ref, vn_ref), valid)


def _attn_sample(q, k, v, cache_k, cache_v, sinks, row0, t_new):
    b, n_past, _ = cache_k.shape
    assert n_past == WINDOW and t_new & (t_new - 1) == 0
    bb = SAMPLE_BATCH_BLOCK
    rows = bb * t_new
    assert row0 % rows == 0 and rows <= NEW_KEY_PAD
    tok = lambda i: (row0 // rows + i, 0)
    i3 = lambda i: (i, 0, 0)
    return pl.pallas_call(
        _attn_sample_kernel,
        grid=(b // bb,),
        in_specs=[pl.BlockSpec((N_KV_HEADS, N_GROUP * rows, V7X_LANES), lambda i: (0, 0, 0)),
                  pl.BlockSpec((rows, Q_DIM), tok),
                  pl.BlockSpec((rows, KV_DIM), tok),
                  pl.BlockSpec((rows, KV_DIM), tok),
                  pl.BlockSpec((bb, n_past, KV_DIM), i3),
                  pl.BlockSpec((bb, n_past, KV_DIM), i3)],
        out_specs=[pl.BlockSpec((rows, Q_DIM), lambda i: (i, 0)),
                   pl.BlockSpec((bb, n_past, KV_DIM), i3),
                   pl.BlockSpec((bb, n_past, KV_DIM), i3)],
        out_shape=[jax.ShapeDtypeStruct((b * t_new, Q_DIM), BF16),
                   jax.ShapeDtypeStruct((b, n_past, KV_DIM), F32),
                   jax.ShapeDtypeStruct((b, n_past, KV_DIM), F32)],
        compiler_params=_params("arbitrary"),
        name="attn_sample",
    )(_sink_table(sinks, rows), q, k, v, cache_k, cache_v)


def _ln_swish(c, g, b):
    mu = jnp.mean(c, axis=-1, keepdims=True)
    var = jnp.mean(jnp.square(c - mu), axis=-1, keepdims=True)
    y = (c - mu) * lax.rsqrt(var + EPS) * g + b
    return jax.nn.silu(y)


CONV_HALO = 32
SUBLANES = 8
CONV_ROW_TILE = 128


def _conv_prompt_kernel(cur_ref, halo_ref, w_ref, b_ref, g_ref, bl_ref, side_ref, o_ref, side_o_ref,
                        xs_ref, c_ref):
    i = pl.program_id(0)
    tt = cur_ref.shape[0]
    side_o_ref[...] = side_ref[...].astype(side_o_ref.dtype)
    xs_ref[0:CONV_HALO, :] = jnp.where(i > 0, halo_ref[...], 0.0)
    xs_ref[CONV_HALO:CONV_HALO + tt, :] = cur_ref[...]
    off = CONV_HALO - (CONV_WIDTH - 1)
    n_groups = CONV_ROW_TILE // SUBLANES
    taps_by_shift = [[(w, (off + w) // SUBLANES) for w in range(CONV_WIDTH) if (off + w) % SUBLANES == s]
                     for s in range(SUBLANES)]
    n_in = max(n_groups + max(d for _, d in taps) + (s > 0) for s, taps in enumerate(taps_by_shift))
    assert tt - CONV_ROW_TILE + n_in * SUBLANES <= CONV_HALO + tt
    sub = lax.broadcasted_iota(jnp.int32, (SUBLANES, V7X_LANES), 0)

    def lane_chunk(cc, carry):
        ls = pl.ds(pl.multiple_of(cc * V7X_LANES, V7X_LANES), V7X_LANES)
        for r0 in range(0, tt, CONV_ROW_TILE):
            x0 = [xs_ref[pl.ds(r0 + kg * SUBLANES, SUBLANES), ls] for kg in range(n_in)]
            acc = [None] * n_groups
            for s, taps in enumerate(taps_by_shift):
                wv = {w: jnp.broadcast_to(w_ref[w:w + 1, ls], (SUBLANES, V7X_LANES)) for w, _ in taps}
                n_used = n_groups + max(d for _, d in taps)
                if s == 0:
                    xsft = x0
                else:
                    rot = [pltpu.roll(x0[kg], SUBLANES - s, axis=0) for kg in range(n_used + 1)]
                    xsft = [jnp.where(sub < SUBLANES - s, rot[kg], rot[kg + 1]) for kg in range(n_used)]
                for kg in range(n_used):
                    for w, d in taps:
                        og = kg - d
                        if 0 <= og < n_groups:
                            term = xsft[kg] * wv[w]
                            acc[og] = term if acc[og] is None else acc[og] + term
            bias = b_ref[:, ls]
            for og in range(n_groups):
                c_ref[pl.ds(r0 + og * SUBLANES, SUBLANES), ls] = acc[og] + bias
        return carry

    lax.fori_loop(0, cur_ref.shape[1] // V7X_LANES, lane_chunk, 0)
    o_ref[...] = _ln_swish(c_ref[...], g_ref[...], bl_ref[...]).astype(o_ref.dtype)


def _conv_prompt(cu, w_dw, b_dw, g_ln, b_ln, t_prompt, make_side_cast):
    tt = 256
    c = cu.shape[1]
    per = tt // CONV_HALO
    row1 = lambda i: (0, 0)
    side = make_side_cast(t_prompt // tt)
    return pl.pallas_call(
        _conv_prompt_kernel,
        grid=(t_prompt // tt,),
        in_specs=[pl.BlockSpec((tt, c), lambda i: (i, 0)),
                  pl.BlockSpec((CONV_HALO, c), lambda i: (jnp.maximum(i * per - 1, 0), 0)),
                  pl.BlockSpec((CONV_WIDTH, c), row1),
                  pl.BlockSpec((1, c), row1),
                  pl.BlockSpec((1, c), row1),
                  pl.BlockSpec((1, c), row1),
                  pl.BlockSpec(side.slab, lambda i: side.in_map(0, i))],
        out_specs=[pl.BlockSpec((tt, c), lambda i: (i, 0)),
                   pl.BlockSpec(side.slab, lambda i: side.out_map(0, i))],
        out_shape=[jax.ShapeDtypeStruct((t_prompt, c), BF16),
                   jax.ShapeDtypeStruct(side.out_shape, BF16)],
        scratch_shapes=[pltpu.VMEM((CONV_HALO + tt, c), F32), pltpu.VMEM((tt, c), F32)],
        compiler_params=_params("arbitrary"),
        name="conv_prompt",
    )(cu, cu, w_dw, b_dw.reshape(1, c), g_ln.reshape(1, c), b_ln.reshape(1, c), side.array)


def _conv_sample_kernel(st_ref, cu_ref, w_ref, b_ref, g_ref, bl_ref, o_ref, so_ref, xs_ref):
    bb, n_hist, c = st_ref.shape
    t_new = cu_ref.shape[0] // bb
    for b in range(bb):
        xs_ref[0:n_hist, :] = st_ref[b]
        xs_ref[n_hist:n_hist + t_new, :] = cu_ref[b * t_new:(b + 1) * t_new, :]
        so_ref[b] = xs_ref[t_new:t_new + n_hist, :]
        acc = xs_ref[0:t_new, :] * w_ref[0:1, :]
        for w in range(1, CONV_WIDTH):
            acc = acc + xs_ref[w:w + t_new, :] * w_ref[w:w + 1, :]
        o_ref[b * t_new:(b + 1) * t_new, :] = _ln_swish(
            acc + b_ref[...], g_ref[...], bl_ref[...]).astype(o_ref.dtype)


def _conv_sample(cu, state, w_dw, b_dw, g_ln, b_ln, row0, t_new):
    b, n_hist, c = state.shape
    bb = SAMPLE_BATCH_BLOCK
    rows = bb * t_new
    assert row0 % rows == 0
    i3 = lambda i: (i, 0, 0)
    row1 = lambda i: (0, 0)
    return pl.pallas_call(
        _conv_sample_kernel,
        grid=(b // bb,),
        in_specs=[pl.BlockSpec((bb, n_hist, c), i3),
                  pl.BlockSpec((rows, c), lambda i: (row0 // rows + i, 0)),
                  pl.BlockSpec((CONV_WIDTH, c), row1),
                  pl.BlockSpec((1, c), row1),
                  pl.BlockSpec((1, c), row1),
                  pl.BlockSpec((1, c), row1)],
        out_specs=[pl.BlockSpec((rows, c), lambda i: (i, 0)),
                   pl.BlockSpec((bb, n_hist, c), i3)],
        out_shape=[jax.ShapeDtypeStruct((b * t_new, c), BF16),
                   jax.ShapeDtypeStruct((b, n_hist, c), F32)],
        scratch_shapes=[pltpu.VMEM((n_hist + t_new + 6, c), F32)],
        compiler_params=_params("arbitrary"),
        name="conv_sample",
    )(state, cu, w_dw, b_dw.reshape(1, c), g_ln.reshape(1, c), b_ln.reshape(1, c))


def _rope_tables(pos):
    half = HEAD_DIM // 2
    inv_freq = ROPE_THETA ** (-jnp.arange(half, dtype=F32) / half)
    ang = pos.astype(F32)[:, None] * inv_freq[None, :]
    cos, sin = jnp.cos(ang), jnp.sin(ang)
    cos64 = jnp.concatenate([cos, cos], axis=-1)
    sin64 = jnp.concatenate([-sin, sin], axis=-1)
    reps = V7X_LANES // HEAD_DIM
    return jnp.tile(cos64, (1, reps)), jnp.tile(sin64, (1, reps))


BIG_TOKEN_TILE = 1088
FFN_TOKEN_TILE = 2176
FFN_BLOCK = 256


def kernel(x_prompt, x_sample, cache_k, cache_v, state_conv, g_mix_norm, w_in, sinks, w_attn_o,
           w_dw, b_dw, g_conv_ln, b_conv_ln, w_conv_o, w_out, g_ffn_norm, w_ffn_in, w_ffn_out,
           g_final):
    depth = w_in.shape[0]
    assert depth == 1, "single-layer trunk"
    t_p = x_prompt.shape[1]
    b_s, t_s, _ = x_sample.shape
    m = t_p + b_s * t_s
    tm = TOKEN_TILE
    bw = WEIGHT_BLOCK
    sds = jax.ShapeDtypeStruct
    plain = _cast_plain

    x = Stacked(x_prompt.reshape(t_p, D_MODEL), x_sample.reshape(b_s * t_s, D_MODEL))
    pos = jnp.concatenate([jnp.arange(t_p, dtype=jnp.int32),
                           jnp.tile(PAST_LEN + jnp.arange(t_s, dtype=jnp.int32), b_s)])
    cos_t, sin_t = _rope_tables(pos)
    rope_extras = [(cos_t, (tm, V7X_LANES), lambda j, i: (i, 0)),
                   (sin_t, (tm, V7X_LANES), lambda j, i: (i, 0))]

    l = 0
    w_in_l = w_in[l]
    h = _rmsnorm(x, g_mix_norm[l], BF16, "rmsnorm_mix")

    kv_blk0 = Q_DIM // bw
    ua_blk0 = (Q_DIM + 2 * KV_DIM) // bw
    ub_blk0 = ua_blk0 + CONV_DIM // bw
    ga_blk0 = ub_blk0 + CONV_DIM // bw
    assert 2 * KV_DIM == bw and N_GROUP * HEAD_DIM == bw

    (q,) = _wres_matmul(
        [(h, w_in_l, bw, [lambda j: 2 * j, lambda j: 2 * j + 1], _cast_interleave_heads)],
        rope_extras,
        [(sds((m, Q_DIM), BF16), (tm, 2 * bw), lambda j, i: (i, j))],
        _epi_q, n_col_steps=N_KV_PAIRS, name="proj_q")

    k_all, v_all = _wres_matmul(
        [(h, w_in_l, bw, [lambda j: kv_blk0], plain)],
        rope_extras,
        [(sds((m, KV_DIM), F32), (tm, KV_DIM), lambda j, i: (i, 0)),
         (sds((m, KV_DIM), F32), (tm, KV_DIM), lambda j, i: (i, 0))],
        _epi_kv, n_col_steps=1, name="proj_kv")

    (cu,) = _wres_matmul(
        [(h, w_in_l, bw, [lambda j: ua_blk0 + j, lambda j: ub_blk0 + j], plain)],
        [],
        [(sds((m, CONV_DIM), F32), (tm, bw), lambda j, i: (i, j))],
        _epi_glu, n_col_steps=CONV_DIM // bw, name="proj_glu")

    n_past = cache_k.shape[2]
    gate_col0 = ga_blk0 * bw
    o_p, w_gates_bf = _attn_prompt(
        q, k_all, v_all, sinks[l], t_p,
        lambda n_steps: _slab_stream(w_in_l, (D_MODEL // 4, bw), 1, n_steps, col0=gate_col0))
    o_s, k_win, v_win = _attn_sample(q, k_all, v_all,
                                     cache_k[l].reshape(b_s, n_past, KV_DIM),
                                     cache_v[l].reshape(b_s, n_past, KV_DIM), sinks[l], t_p, t_s)

    c_p, w_ffn_in_bf = _conv_prompt(
        cu, w_dw[l], b_dw[l], g_conv_ln[l], b_conv_ln[l], t_p,
        lambda n_steps: _slab_stream(w_ffn_in[l], (D_MODEL // 8, 2 * D_FF // 4), 1, n_steps))
    c_s, conv_state = _conv_sample(cu, state_conv[l], w_dw[l], b_dw[l], g_conv_ln[l], b_conv_ln[l],
                                   t_p, t_s)

    btm = BIG_TOKEN_TILE
    gate_tn = 2 * bw
    (gates,) = _wres_matmul(
        [(h, w_gates_bf, gate_tn, [lambda j: j], None)],
        [],
        [(sds((m, 2 * D_MODEL), BF16), (btm, gate_tn), lambda j, i: (i, j))],
        _epi_sigmoid, n_col_steps=2 * D_MODEL // gate_tn, name="proj_gates", tm=btm)

    n_gate_blk = D_MODEL // (2 * bw)
    two_blocks = [lambda j: 2 * j, lambda j: 2 * j + 1]
    (mix,) = _wres_matmul(
        [(Stacked(o_p, o_s), w_attn_o[l], bw, two_blocks, _cast_permute_head_rows),
         (Stacked(c_p, c_s), w_conv_o[l], bw, two_blocks, plain)],
        [(gates, (tm, 2 * bw), lambda j, i: (i, j)),
         (gates, (tm, 2 * bw), lambda j, i: (i, j + n_gate_blk))],
        [(sds((m, D_MODEL), BF16), (tm, 2 * bw), lambda j, i: (i, j))],
        _epi_merge, n_col_steps=n_gate_blk, name="merge", weight_buffers=1)

    (x1,) = _wres_matmul(
        [(mix, w_out[l], bw, two_blocks, plain)],
        [(x, (tm, 2 * bw), lambda j, i: (i, j))],
        [(sds((m, D_MODEL), F32), (tm, 2 * bw), lambda j, i: (i, j))],
        _epi_residual, n_col_steps=D_MODEL // (2 * bw), name="out_proj", weight_buffers=1)

    h2 = _rmsnorm(x1, g_ffn_norm[l], BF16, "rmsnorm_ffn")
    ftm = FFN_TOKEN_TILE
    n_ff_blk = D_FF // FFN_BLOCK
    n_ff_steps = n_ff_blk * (m // ftm)
    act, w_ffn_out_bf = _wres_matmul(
        [(h2, w_ffn_in_bf, FFN_BLOCK, [lambda j: j, lambda j: j + n_ff_blk], None)],
        [],
        [(sds((m, D_FF), BF16), (ftm, FFN_BLOCK), lambda j, i: (i, j))],
        _epi_swiglu, n_col_steps=n_ff_blk, name="ffn_in", tm=ftm,
        side_casts=[_slab_stream(w_ffn_out[l], (D_FF // n_ff_steps, D_MODEL), n_ff_blk, m // ftm)])

    x2 = _ares_matmul_residual(act, w_ffn_out_bf, x1, tm=tm, tn=512, name="ffn_out")
    norm_tile = 256
    y_p = _rmsnorm(x2, g_final, F32, "rmsnorm_final_prompt", row_tile0=0, n_tiles=t_p // norm_tile,
                   tm=norm_tile)
    y_s = _rmsnorm(x2, g_final, F32, "rmsnorm_final_sample", row_tile0=t_p // norm_tile,
                   n_tiles=(m - t_p) // norm_tile, tm=norm_tile)

    y_prompt = y_p.reshape(x_prompt.shape)
    y_sample = y_s.reshape(x_sample.shape)
    k_prompt = k_all[t_p - WINDOW:t_p].reshape(1, 1, WINDOW, N_KV_HEADS, HEAD_DIM)
    v_prompt = v_all[t_p - WINDOW:t_p].reshape(1, 1, WINDOW, N_KV_HEADS, HEAD_DIM)
    conv_prompt = cu[t_p - (CONV_WIDTH - 1):t_p].reshape(1, 1, CONV_WIDTH - 1, CONV_DIM)
    k_sample = k_win.reshape(1, b_s, WINDOW, N_KV_HEADS, HEAD_DIM)
    v_sample = v_win.reshape(1, b_s, WINDOW, N_KV_HEADS, HEAD_DIM)
    conv_sample = conv_state.reshape(1, b_s, CONV_WIDTH - 1, CONV_DIM)
    return (y_prompt, y_sample, k_prompt, v_prompt, conv_prompt, k_sample, v_sample, conv_sample)
```

```python
import collections
import functools

import jax
import jax.numpy as jnp
from jax import lax
from jax.experimental import pallas as pl
from jax.experimental.pallas import tpu as pltpu

D_MODEL = 4096
HEAD_DIM = 64
N_HEADS = 32
N_KV_HEADS = 4
N_GROUP = N_HEADS // N_KV_HEADS
WINDOW = 128
ROPE_THETA = 10000.0
CONV_DIM = 2048
CONV_WIDTH = 31
D_FF = 11008
PAST_LEN = 8192
Q_DIM = N_HEADS * HEAD_DIM
KV_DIM = N_KV_HEADS * HEAD_DIM
EPS = 1e-6
NEG = -1e30

V7X_LANES = 128
V7X_VMEM_LIMIT_BYTES = 60000 * 1024

F32 = jnp.float32
BF16 = jnp.bfloat16

TOKEN_TILE = 512
WEIGHT_BLOCK = 512
CAST_ROWS = 256
N_KV_PAIRS = N_KV_HEADS // 2
CHUNKS_PER_PAIR = N_GROUP

Stacked = collections.namedtuple("Stacked", ["prompt", "sample"])

SideCast = collections.namedtuple("SideCast", ["array", "slab", "in_map", "out_shape", "out_map"])


def _slab_stream(array, slab, n_outer, n_inner, *, col0=0, n_cols=None, row_of_slab=None):
    k, n = array.shape
    n_cols = n - col0 if n_cols is None else n_cols
    assert k % slab[0] == 0 and n_cols % slab[1] == 0 and col0 % slab[1] == 0
    n_r, n_c = k // slab[0], n_cols // slab[1]
    assert n_r * n_c <= n_outer * n_inner, "not enough grid steps to cast every slab"

    def rc(j, i):
        t = jnp.minimum(j * n_inner + i, n_r * n_c - 1)
        return t // n_c, t % n_c

    def in_map(j, i):
        r, c = rc(j, i)
        return (r if row_of_slab is None else row_of_slab(r), col0 // slab[1] + c)

    return SideCast(array, slab, in_map, (k, n_cols), rc)


def _params(*sem):
    return pltpu.CompilerParams(dimension_semantics=sem,
                                vmem_limit_bytes=V7X_VMEM_LIMIT_BYTES)


def _stacked_specs(op, block, imap, tm):
    if not isinstance(op, Stacked):
        return [op], [pl.BlockSpec(block, imap)]
    n_p = op.prompt.shape[0] // tm
    assert op.prompt.shape[0] % tm == 0 and op.sample.shape[0] % tm == 0

    def p_map(*ids):
        r, c = imap(*ids)
        return (jnp.minimum(r, n_p - 1), c)

    def s_map(*ids):
        r, c = imap(*ids)
        return (jnp.maximum(r - n_p, 0), c)

    return [op.prompt, op.sample], [pl.BlockSpec(block, p_map), pl.BlockSpec(block, s_map)]


def _n_prompt_tiles(op, tm):
    return op.prompt.shape[0] // tm if isinstance(op, Stacked) else None


def _load(refs, n_p, row_tile):
    if n_p is None:
        return refs[0][...]
    return jnp.where(row_tile < n_p, refs[0][...], refs[1][...])


def _rmsnorm_kernel(*refs, n_p):
    x_refs, (g_ref, o_ref) = refs[:-2], refs[-2:]
    x = _load(x_refs, n_p, pl.program_id(0))
    y = x * lax.rsqrt(jnp.mean(x * x, axis=-1, keepdims=True) + EPS) * g_ref[...]
    o_ref[...] = y.astype(o_ref.dtype)


def _rmsnorm(x, g, out_dtype, name, *, row_tile0=0, n_tiles=None, tm=256):
    d = g.shape[0]
    rows = (x.prompt.shape[0] + x.sample.shape[0]) if isinstance(x, Stacked) else x.shape[0]
    n_tiles = rows // tm if n_tiles is None else n_tiles
    arrays, specs = _stacked_specs(x, (tm, d), lambda i: (i + row_tile0, 0), tm)
    return pl.pallas_call(
        functools.partial(_rmsnorm_kernel, n_p=_n_prompt_tiles(x, tm)),
        grid=(n_tiles,),
        in_specs=specs + [pl.BlockSpec((1, d), lambda i: (0, 0))],
        out_specs=pl.BlockSpec((tm, d), lambda i: (i, 0)),
        out_shape=jax.ShapeDtypeStruct((n_tiles * tm, d), out_dtype),
        compiler_params=_params("arbitrary"),
        name=name,
    )(*arrays, g.reshape(1, d))


def _cast_plain(w_refs, wbf_ref):
    k, bw = w_refs[0].shape

    def body(c, carry):
        r = pl.multiple_of(c * CAST_ROWS, CAST_ROWS)
        for b, w_ref in enumerate(w_refs):
            wbf_ref[pl.ds(r, CAST_ROWS), b * bw:(b + 1) * bw] = w_ref[pl.ds(r, CAST_ROWS), :].astype(BF16)
        return carry

    lax.fori_loop(0, k // CAST_ROWS, body, 0)


def _cast_interleave_heads(w_refs, wbf_ref):
    w_even, w_odd = w_refs
    k, bw = w_even.shape

    def body(c, carry):
        r = pl.multiple_of(c * CAST_ROWS, CAST_ROWS)
        a = w_even[pl.ds(r, CAST_ROWS), :]
        b = w_odd[pl.ds(r, CAST_ROWS), :]
        pieces = []
        for g in range(bw // HEAD_DIM):
            pieces += [a[:, g * HEAD_DIM:(g + 1) * HEAD_DIM], b[:, g * HEAD_DIM:(g + 1) * HEAD_DIM]]
        wbf_ref[pl.ds(r, CAST_ROWS), :] = jnp.concatenate(pieces, axis=1).astype(BF16)
        return carry

    lax.fori_loop(0, k // CAST_ROWS, body, 0)


def _interleaved_head(slot):
    chunk, parity = divmod(slot, 2)
    pair, g = divmod(chunk, CHUNKS_PER_PAIR)
    return (2 * pair + parity) * N_GROUP + g


def _wres_kernel(*refs, layout, casts, epilogue):
    a_counts, w_counts, extra_counts, n_out, n_side, a_np, extra_np = layout
    pos = 0

    def take(counts):
        nonlocal pos
        groups = []
        for n in counts:
            groups.append(refs[pos:pos + n])
            pos += n
        return groups

    a_groups = take(a_counts)
    w_groups = take(w_counts)
    extra_groups = take(extra_counts)
    side_in = refs[pos:pos + n_side]
    pos += n_side
    out_refs = refs[pos:pos + n_out]
    pos += n_out
    side_out = refs[pos:pos + n_side]
    scratch_refs = list(refs[pos + n_side:])
    wbf_refs = [scratch_refs.pop(0) if cast is not None else None for cast in casts]
    i = pl.program_id(1)

    if any(cast is not None for cast in casts):
        @pl.when(i == 0)
        def _():
            for cast, w_refs, wbf_ref in zip(casts, w_groups, wbf_refs):
                if cast is not None:
                    cast(w_refs, wbf_ref)

    for s_in, s_out in zip(side_in, side_out):
        s_out[...] = s_in[...].astype(s_out.dtype)

    accs = []
    for a_refs, n_p, w_refs, wbf_ref in zip(a_groups, a_np, w_groups, wbf_refs):
        a = _load(a_refs, n_p, i)
        if wbf_ref is not None:
            accs.append(jnp.dot(a, wbf_ref[...], preferred_element_type=F32))
        else:
            parts = [jnp.dot(a, w_ref[...], preferred_element_type=F32) for w_ref in w_refs]
            accs.append(parts[0] if len(parts) == 1 else jnp.concatenate(parts, axis=1))
    extras = [_load(e_refs, n_p, i) for e_refs, n_p in zip(extra_groups, extra_np)]
    for o_ref, o in zip(out_refs, epilogue(accs, extras)):
        o_ref[...] = o.astype(o_ref.dtype)


def _wres_matmul(terms, extras, outs, epilogue, *, n_col_steps, name, tm=TOKEN_TILE,
                 weight_buffers=2, side_casts=()):
    a0 = terms[0][0]
    m = (a0.prompt.shape[0] + a0.sample.shape[0]) if isinstance(a0, Stacked) else a0.shape[0]
    assert m % tm == 0
    in_arrays, in_specs, scratch, casts = [], [], [], []
    a_counts, w_counts, extra_counts, a_np, extra_np = [], [], [], [], []
    for a, w, bw, col_fns, cast in terms:
        k = (w[0] if isinstance(w, tuple) else w).shape[0]
        arrs, specs = _stacked_specs(a, (tm, k), lambda j, i: (i, 0), tm)
        in_arrays += arrs
        in_specs += specs
        a_counts.append(len(arrs))
        a_np.append(_n_prompt_tiles(a, tm))
    for a, w, bw, col_fns, cast in terms:
        w_of_block = w if isinstance(w, tuple) else (w,) * len(col_fns)
        k = w_of_block[0].shape[0]
        for fn, w_blk in zip(col_fns, w_of_block):
            in_arrays.append(w_blk)
            in_specs.append(pl.BlockSpec((k, bw), functools.partial(lambda j, i, fn: (0, fn(j)), fn=fn),
                                         pipeline_mode=pl.Buffered(weight_buffers)))
        w_counts.append(len(col_fns))
        casts.append(cast)
        if cast is not None:
            scratch.append(pltpu.VMEM((k, bw * len(col_fns)), BF16))
    for op, blk, imap in extras:
        arrs, specs = _stacked_specs(op, blk, imap, tm)
        in_arrays += arrs
        in_specs += specs
        extra_counts.append(len(arrs))
        extra_np.append(_n_prompt_tiles(op, tm))
    for sc in side_casts:
        in_arrays.append(sc.array)
        in_specs.append(pl.BlockSpec(sc.slab, sc.in_map))
    layout = (tuple(a_counts), tuple(w_counts), tuple(extra_counts), len(outs), len(side_casts),
              tuple(a_np), tuple(extra_np))
    return pl.pallas_call(
        functools.partial(_wres_kernel, layout=layout, casts=tuple(casts), epilogue=epilogue),
        grid=(n_col_steps, m // tm),
        in_specs=in_specs,
        out_specs=([pl.BlockSpec(blk, imap) for _, blk, imap in outs]
                   + [pl.BlockSpec(sc.slab, sc.out_map) for sc in side_casts]),
        out_shape=([s for s, _, _ in outs]
                   + [jax.ShapeDtypeStruct(sc.out_shape, BF16) for sc in side_casts]),
        scratch_shapes=scratch,
        compiler_params=_params("arbitrary", "arbitrary"),
        name=name,
    )(*in_arrays)


def _ares_kernel(a_ref, w_ref, x_ref, o_ref):
    acc = jnp.dot(a_ref[...], w_ref[...], preferred_element_type=F32)
    o_ref[...] = x_ref[...] + acc


def _ares_matmul_residual(a, w_bf, x, *, tm, tn, name):
    m, k = a.shape
    n = w_bf.shape[1]
    return pl.pallas_call(
        _ares_kernel,
        grid=(m // tm, n // tn),
        in_specs=[pl.BlockSpec((tm, k), lambda i, j: (i, 0)),
                  pl.BlockSpec((k, tn), lambda i, j: (0, j)),
                  pl.BlockSpec((tm, tn), lambda i, j: (i, j))],
        out_specs=pl.BlockSpec((tm, tn), lambda i, j: (i, j)),
        out_shape=jax.ShapeDtypeStruct((m, n), F32),
        compiler_params=_params("arbitrary", "arbitrary"),
        name=name,
    )(a, w_bf, x)


def _rope(x, cos128, sin128):
    n = x.shape[1]
    reps = n // V7X_LANES
    c = jnp.concatenate([cos128] * reps, axis=1) if reps > 1 else cos128
    s = jnp.concatenate([sin128] * reps, axis=1) if reps > 1 else sin128
    lane = lax.broadcasted_iota(jnp.int32, x.shape, 1)
    first_half = (lane & (HEAD_DIM - 1)) < (HEAD_DIM // 2)
    partner = jnp.where(first_half,
                        pltpu.roll(x, n - HEAD_DIM // 2, axis=1),
                        pltpu.roll(x, HEAD_DIM // 2, axis=1))
    return x * c + partner * s


def _epi_q(accs, extras):
    cos, sin = extras
    return [_rope(accs[0], cos, sin)]


def _epi_kv(accs, extras):
    cos, sin = extras
    acc = accs[0]
    return [_rope(acc[:, :KV_DIM], cos, sin), acc[:, KV_DIM:]]


def _epi_glu(accs, extras):
    acc = accs[0]
    half = acc.shape[1] // 2
    return [acc[:, :half] * jax.nn.sigmoid(acc[:, half:])]


def _epi_sigmoid(accs, extras):
    return [jax.nn.sigmoid(accs[0])]


def _epi_merge(accs, extras):
    g_a, g_c = extras
    return [g_a.astype(F32) * accs[0] + g_c.astype(F32) * accs[1]]


def _epi_residual(accs, extras):
    return [extras[0] + accs[0]]


def _epi_swiglu(accs, extras):
    acc = accs[0]
    half = acc.shape[1] // 2
    return [jax.nn.silu(acc[:, :half]) * acc[:, half:]]


def _attend(qs, kx, vx_ones, valid, sk, n_rep):
    r, nk = qs.shape[0], kx.shape[0]
    s = lax.dot_general(qs, kx, (((1,), (1,)), ((), ())), preferred_element_type=F32)
    s = jnp.where(valid[None], s.reshape(n_rep, r // n_rep, nk), NEG).reshape(r, nk)
    m = jnp.maximum(jnp.broadcast_to(jnp.max(s, axis=-1, keepdims=True), sk.shape), sk)
    p = jnp.exp(s - jnp.concatenate([m] * (nk // V7X_LANES), axis=1)).astype(BF16)
    ol = jnp.dot(p, vx_ones, preferred_element_type=F32)
    return ol[:, :V7X_LANES] / (ol[:, V7X_LANES:] + jnp.exp(sk - m))


def _attend_all_heads(q_ref, o_ref, sink_ref, k_of_pair, v_of_pair, valid):
    rows = q_ref.shape[0]
    nk = valid.shape[1]
    lane = lax.broadcasted_iota(jnp.int32, (nk, V7X_LANES), 1)
    ones = jnp.ones((nk, V7X_LANES), BF16)
    scale = HEAD_DIM ** -0.5
    for pair in range(N_KV_PAIRS):
        kcol = k_of_pair(pair) * scale
        vcol = v_of_pair(pair)
        c0 = pair * CHUNKS_PER_PAIR
        qs = jnp.concatenate([q_ref[:, (c0 + g) * V7X_LANES:(c0 + g + 1) * V7X_LANES]
                              for g in range(CHUNKS_PER_PAIR)], axis=0)
        o_pair = None
        for parity in range(2):
            half = (lane < HEAD_DIM) if parity == 0 else (lane >= HEAD_DIM)
            kx = jnp.where(half, kcol, 0.0).astype(BF16)
            vx = jnp.concatenate([jnp.where(half, vcol, 0.0).astype(BF16), ones], axis=1)
            o = _attend(qs, kx, vx, valid, sink_ref[2 * pair + parity], CHUNKS_PER_PAIR)
            o_pair = o if o_pair is None else o_pair + o
        for g in range(CHUNKS_PER_PAIR):
            o_ref[:, (c0 + g) * V7X_LANES:(c0 + g + 1) * V7X_LANES] = (
                o_pair[g * rows:(g + 1) * rows].astype(o_ref.dtype))


def _sink_table(sinks, rows):
    per_row = jnp.repeat(sinks.astype(F32).reshape(N_KV_HEADS, N_GROUP), rows, axis=1)
    return jnp.broadcast_to(per_row[:, :, None], (N_KV_HEADS, N_GROUP * rows, V7X_LANES))


def _attn_prompt_kernel(sink_ref, q_ref, kc_ref, kp_ref, vc_ref, vp_ref, o_ref):
    n = pl.program_id(0)
    blk = q_ref.shape[0]
    row = lax.broadcasted_iota(jnp.int32, (blk, 2 * blk), 0)
    col = lax.broadcasted_iota(jnp.int32, (blk, 2 * blk), 1)
    diff = row + blk - col
    valid = (diff >= 0) & (diff < WINDOW) & ((n > 0) | (col >= blk))

    def both_blocks(prev_ref, cur_ref):
        return lambda pair: jnp.concatenate(
            [prev_ref[:, pair * V7X_LANES:(pair + 1) * V7X_LANES],
             cur_ref[:, pair * V7X_LANES:(pair + 1) * V7X_LANES]], axis=0)

    _attend_all_heads(q_ref, o_ref, sink_ref, both_blocks(kp_ref, kc_ref), both_blocks(vp_ref, vc_ref), valid)


def _attn_prompt(q, k, v, sinks, t_prompt):
    blk = WINDOW
    nb = t_prompt // blk
    cur = lambda n: (n, 0)
    prev = lambda n: (jnp.maximum(n - 1, 0), 0)
    return pl.pallas_call(
        _attn_prompt_kernel,
        grid=(nb,),
        in_specs=[pl.BlockSpec((N_KV_HEADS, N_GROUP * blk, V7X_LANES), lambda n: (0, 0, 0)),
                  pl.BlockSpec((blk, Q_DIM), cur),
                  pl.BlockSpec((blk, KV_DIM), cur),
                  pl.BlockSpec((blk, KV_DIM), prev),
                  pl.BlockSpec((blk, KV_DIM), cur),
                  pl.BlockSpec((blk, KV_DIM), prev)],
        out_specs=pl.BlockSpec((blk, Q_DIM), cur),
        out_shape=jax.ShapeDtypeStruct((t_prompt, Q_DIM), BF16),
        compiler_params=_params("arbitrary"),
        name="attn_prompt",
    )(_sink_table(sinks, blk), q, k, k, v, v)


SAMPLE_BATCH_BLOCK = 8
NEW_KEY_PAD = 128


def _attn_sample_kernel(sink_ref, q_ref, kn_ref, vn_ref, ck_ref, cv_ref, o_ref, ko_ref, vo_ref):
    bb, n_past = ck_ref.shape[0], ck_ref.shape[1]
    rows = q_ref.shape[0]
    t_new = rows // bb
    n_cache = bb * n_past
    nk = n_cache + NEW_KEY_PAD
    lg_t, lg_p = t_new.bit_length() - 1, n_past.bit_length() - 1

    for b in range(bb):
        ko_ref[b, 0:n_past - t_new, :] = ck_ref[b, t_new:n_past, :]
        ko_ref[b, n_past - t_new:n_past, :] = kn_ref[b * t_new:(b + 1) * t_new, :]
        vo_ref[b, 0:n_past - t_new, :] = cv_ref[b, t_new:n_past, :]
        vo_ref[b, n_past - t_new:n_past, :] = vn_ref[b * t_new:(b + 1) * t_new, :]

    r = lax.broadcasted_iota(jnp.int32, (rows, nk), 0)
    c = lax.broadcasted_iota(jnp.int32, (rows, nk), 1)
    b_r, t = r >> lg_t, r & (t_new - 1)
    d_cache = t + n_past - (c & (n_past - 1))
    valid_cache = (c < n_cache) & ((c >> lg_p) == b_r) & (d_cache >= 0) & (d_cache < WINDOW)
    cn = c - n_cache
    d_new = t - (cn & (t_new - 1))
    valid_new = (cn >= 0) & (cn < rows) & ((cn >> lg_t) == b_r) & (d_new >= 0) & (d_new < WINDOW)
    valid = valid_cache | valid_new

    pad = jnp.zeros((NEW_KEY_PAD - rows, V7X_LANES), F32)

    def keys(cache_ref, new_ref):
        def of_pair(pair):
            ls = slice(pair * V7X_LANES, (pair + 1) * V7X_LANES)
            return jnp.concatenate([cache_ref[:, :, ls].reshape(n_cache, V7X_LANES), new_ref[:, ls], pad], axis=0)
        return of_pair

    _attend_all_heads(q_ref, o_ref, sink_ref, keys(ck_ref, kn_ref), keys(cv_ref, vn_ref), valid)


def _attn_sample(q, k, v, cache_k, cache_v, sinks, row0, t_new):
    b, n_past, _ = cache_k.shape
    assert n_past == WINDOW and t_new & (t_new - 1) == 0
    bb = SAMPLE_BATCH_BLOCK
    rows = bb * t_new
    assert row0 % rows == 0 and rows <= NEW_KEY_PAD
    tok = lambda i: (row0 // rows + i, 0)
    i3 = lambda i: (i, 0, 0)
    return pl.pallas_call(
        _attn_sample_kernel,
        grid=(b // bb,),
        in_specs=[pl.BlockSpec((N_KV_HEADS, N_GROUP * rows, V7X_LANES), lambda i: (0, 0, 0)),
                  pl.BlockSpec((rows, Q_DIM), tok),
                  pl.BlockSpec((rows, KV_DIM), tok),
                  pl.BlockSpec((rows, KV_DIM), tok),
                  pl.BlockSpec((bb, n_past, KV_DIM), i3),
                  pl.BlockSpec((bb, n_past, KV_DIM), i3)],
        out_specs=[pl.BlockSpec((rows, Q_DIM), lambda i: (i, 0)),
                   pl.BlockSpec((bb, n_past, KV_DIM), i3),
                   pl.BlockSpec((bb, n_past, KV_DIM), i3)],
        out_shape=[jax.ShapeDtypeStruct((b * t_new, Q_DIM), BF16),
                   jax.ShapeDtypeStruct((b, n_past, KV_DIM), F32),
                   jax.ShapeDtypeStruct((b, n_past, KV_DIM), F32)],
        compiler_params=_params("arbitrary"),
        name="attn_sample",
    )(_sink_table(sinks, rows), q, k, v, cache_k, cache_v)


def _ln_swish(c, g, b):
    mu = jnp.mean(c, axis=-1, keepdims=True)
    var = jnp.mean(jnp.square(c - mu), axis=-1, keepdims=True)
    y = (c - mu) * lax.rsqrt(var + EPS) * g + b
    return jax.nn.silu(y)


CONV_HALO = 32
SUBLANES = 8
CONV_ROW_TILE = 128


def _conv_prompt_kernel(cur_ref, halo_ref, w_ref, b_ref, g_ref, bl_ref, side_ref, o_ref, side_o_ref,
                        xs_ref, c_ref):
    i = pl.program_id(0)
    tt = cur_ref.shape[0]
    side_o_ref[...] = side_ref[...].astype(side_o_ref.dtype)
    xs_ref[0:CONV_HALO, :] = jnp.where(i > 0, halo_ref[...], 0.0)
    xs_ref[CONV_HALO:CONV_HALO + tt, :] = cur_ref[...]
    off = CONV_HALO - (CONV_WIDTH - 1)
    n_groups = CONV_ROW_TILE // SUBLANES
    taps_by_shift = [[(w, (off + w) // SUBLANES) for w in range(CONV_WIDTH) if (off + w) % SUBLANES == s]
                     for s in range(SUBLANES)]
    n_in = max(n_groups + max(d for _, d in taps) + (s > 0) for s, taps in enumerate(taps_by_shift))
    assert tt - CONV_ROW_TILE + n_in * SUBLANES <= CONV_HALO + tt
    sub = lax.broadcasted_iota(jnp.int32, (SUBLANES, V7X_LANES), 0)

    def lane_chunk(cc, carry):
        ls = pl.ds(pl.multiple_of(cc * V7X_LANES, V7X_LANES), V7X_LANES)
        for r0 in range(0, tt, CONV_ROW_TILE):
            x0 = [xs_ref[pl.ds(r0 + kg * SUBLANES, SUBLANES), ls] for kg in range(n_in)]
            acc = [None] * n_groups
            for s, taps in enumerate(taps_by_shift):
                wv = {w: jnp.broadcast_to(w_ref[w:w + 1, ls], (SUBLANES, V7X_LANES)) for w, _ in taps}
                n_used = n_groups + max(d for _, d in taps)
                if s == 0:
                    xsft = x0
                else:
                    rot = [pltpu.roll(x0[kg], SUBLANES - s, axis=0) for kg in range(n_used + 1)]
                    xsft = [jnp.where(sub < SUBLANES - s, rot[kg], rot[kg + 1]) for kg in range(n_used)]
                for kg in range(n_used):
                    for w, d in taps:
                        og = kg - d
                        if 0 <= og < n_groups:
                            term = xsft[kg] * wv[w]
                            acc[og] = term if acc[og] is None else acc[og] + term
            bias = b_ref[:, ls]
            for og in range(n_groups):
                c_ref[pl.ds(r0 + og * SUBLANES, SUBLANES), ls] = acc[og] + bias
        return carry

    lax.fori_loop(0, cur_ref.shape[1] // V7X_LANES, lane_chunk, 0)
    o_ref[...] = _ln_swish(c_ref[...], g_ref[...], bl_ref[...]).astype(o_ref.dtype)


def _conv_prompt(cu, w_dw, b_dw, g_ln, b_ln, t_prompt, make_side_cast):
    tt = 256
    c = cu.shape[1]
    per = tt // CONV_HALO
    row1 = lambda i: (0, 0)
    side = make_side_cast(t_prompt // tt)
    return pl.pallas_call(
        _conv_prompt_kernel,
        grid=(t_prompt // tt,),
        in_specs=[pl.BlockSpec((tt, c), lambda i: (i, 0)),
                  pl.BlockSpec((CONV_HALO, c), lambda i: (jnp.maximum(i * per - 1, 0), 0)),
                  pl.BlockSpec((CONV_WIDTH, c), row1),
                  pl.BlockSpec((1, c), row1),
                  pl.BlockSpec((1, c), row1),
                  pl.BlockSpec((1, c), row1),
                  pl.BlockSpec(side.slab, lambda i: side.in_map(0, i))],
        out_specs=[pl.BlockSpec((tt, c), lambda i: (i, 0)),
                   pl.BlockSpec(side.slab, lambda i: side.out_map(0, i))],
        out_shape=[jax.ShapeDtypeStruct((t_prompt, c), BF16),
                   jax.ShapeDtypeStruct(side.out_shape, BF16)],
        scratch_shapes=[pltpu.VMEM((CONV_HALO + tt, c), F32), pltpu.VMEM((tt, c), F32)],
        compiler_params=_params("arbitrary"),
        name="conv_prompt",
    )(cu, cu, w_dw, b_dw.reshape(1, c), g_ln.reshape(1, c), b_ln.reshape(1, c), side.array)


def _conv_sample_kernel(st_ref, cu_ref, w_ref, b_ref, g_ref, bl_ref, o_ref, so_ref, xs_ref):
    bb, n_hist, c = st_ref.shape
    t_new = cu_ref.shape[0] // bb
    for b in range(bb):
        xs_ref[0:n_hist, :] = st_ref[b]
        xs_ref[n_hist:n_hist + t_new, :] = cu_ref[b * t_new:(b + 1) * t_new, :]
        so_ref[b] = xs_ref[t_new:t_new + n_hist, :]
        acc = xs_ref[0:t_new, :] * w_ref[0:1, :]
        for w in range(1, CONV_WIDTH):
            acc = acc + xs_ref[w:w + t_new, :] * w_ref[w:w + 1, :]
        o_ref[b * t_new:(b + 1) * t_new, :] = _ln_swish(
            acc + b_ref[...], g_ref[...], bl_ref[...]).astype(o_ref.dtype)


def _conv_sample(cu, state, w_dw, b_dw, g_ln, b_ln, row0, t_new):
    b, n_hist, c = state.shape
    bb = SAMPLE_BATCH_BLOCK
    rows = bb * t_new
    assert row0 % rows == 0
    i3 = lambda i: (i, 0, 0)
    row1 = lambda i: (0, 0)
    return pl.pallas_call(
        _conv_sample_kernel,
        grid=(b // bb,),
        in_specs=[pl.BlockSpec((bb, n_hist, c), i3),
                  pl.BlockSpec((rows, c), lambda i: (row0 // rows + i, 0)),
                  pl.BlockSpec((CONV_WIDTH, c), row1),
                  pl.BlockSpec((1, c), row1),
                  pl.BlockSpec((1, c), row1),
                  pl.BlockSpec((1, c), row1)],
        out_specs=[pl.BlockSpec((rows, c), lambda i: (i, 0)),
                   pl.BlockSpec((bb, n_hist, c), i3)],
        out_shape=[jax.ShapeDtypeStruct((b * t_new, c), BF16),
                   jax.ShapeDtypeStruct((b, n_hist, c), F32)],
        scratch_shapes=[pltpu.VMEM((n_hist + t_new + 6, c), F32)],
        compiler_params=_params("arbitrary"),
        name="conv_sample",
    )(state, cu, w_dw, b_dw.reshape(1, c), g_ln.reshape(1, c), b_ln.reshape(1, c))


def _rope_tables(pos):
    half = HEAD_DIM // 2
    inv_freq = ROPE_THETA ** (-jnp.arange(half, dtype=F32) / half)
    ang = pos.astype(F32)[:, None] * inv_freq[None, :]
    cos, sin = jnp.cos(ang), jnp.sin(ang)
    cos64 = jnp.concatenate([cos, cos], axis=-1)
    sin64 = jnp.concatenate([-sin, sin], axis=-1)
    reps = V7X_LANES // HEAD_DIM
    return jnp.tile(cos64, (1, reps)), jnp.tile(sin64, (1, reps))


BIG_TOKEN_TILE = 1088
FFN_TOKEN_TILE = 2176
FFN_BLOCK = 256


def kernel(x_prompt, x_sample, cache_k, cache_v, state_conv, g_mix_norm, w_in, sinks, w_attn_o,
           w_dw, b_dw, g_conv_ln, b_conv_ln, w_conv_o, w_out, g_ffn_norm, w_ffn_in, w_ffn_out,
           g_final):
    depth = w_in.shape[0]
    assert depth == 1, "single-layer trunk"
    t_p = x_prompt.shape[1]
    b_s, t_s, _ = x_sample.shape
    m = t_p + b_s * t_s
    tm = TOKEN_TILE
    bw = WEIGHT_BLOCK
    sds = jax.ShapeDtypeStruct
    plain = _cast_plain

    x = Stacked(x_prompt.reshape(t_p, D_MODEL), x_sample.reshape(b_s * t_s, D_MODEL))
    pos = jnp.concatenate([jnp.arange(t_p, dtype=jnp.int32),
                           jnp.tile(PAST_LEN + jnp.arange(t_s, dtype=jnp.int32), b_s)])
    cos_t, sin_t = _rope_tables(pos)
    rope_extras = [(cos_t, (tm, V7X_LANES), lambda j, i: (i, 0)),
                   (sin_t, (tm, V7X_LANES), lambda j, i: (i, 0))]

    l = 0
    w_in_l = w_in[l]
    h = _rmsnorm(x, g_mix_norm[l], BF16, "rmsnorm_mix")

    kv_blk0 = Q_DIM // bw
    ua_blk0 = (Q_DIM + 2 * KV_DIM) // bw
    ub_blk0 = ua_blk0 + CONV_DIM // bw
    ga_blk0 = ub_blk0 + CONV_DIM // bw
    assert 2 * KV_DIM == bw and N_GROUP * HEAD_DIM == bw

    (q,) = _wres_matmul(
        [(h, w_in_l, bw, [lambda j: 2 * j, lambda j: 2 * j + 1], _cast_interleave_heads)],
        rope_extras,
        [(sds((m, Q_DIM), BF16), (tm, 2 * bw), lambda j, i: (i, j))],
        _epi_q, n_col_steps=N_KV_PAIRS, name="proj_q")

    k_all, v_all = _wres_matmul(
        [(h, w_in_l, bw, [lambda j: kv_blk0], plain)],
        rope_extras,
        [(sds((m, KV_DIM), F32), (tm, KV_DIM), lambda j, i: (i, 0)),
         (sds((m, KV_DIM), F32), (tm, KV_DIM), lambda j, i: (i, 0))],
        _epi_kv, n_col_steps=1, name="proj_kv")

    (cu,) = _wres_matmul(
        [(h, w_in_l, bw, [lambda j: ua_blk0 + j, lambda j: ub_blk0 + j], plain)],
        [],
        [(sds((m, CONV_DIM), F32), (tm, bw), lambda j, i: (i, j))],
        _epi_glu, n_col_steps=CONV_DIM // bw, name="proj_glu")

    n_past = cache_k.shape[2]
    o_p = _attn_prompt(q, k_all, v_all, sinks[l], t_p)
    o_s, k_win, v_win = _attn_sample(q, k_all, v_all,
                                     cache_k[l].reshape(b_s, n_past, KV_DIM),
                                     cache_v[l].reshape(b_s, n_past, KV_DIM), sinks[l], t_p, t_s)

    c_p, w_gates_bf = _conv_prompt(
        cu, w_dw[l], b_dw[l], g_conv_ln[l], b_conv_ln[l], t_p,
        lambda n_steps: _slab_stream(w_in_l, (D_MODEL // 2, bw), 1, n_steps, col0=ga_blk0 * bw))
    c_s, conv_state = _conv_sample(cu, state_conv[l], w_dw[l], b_dw[l], g_conv_ln[l], b_conv_ln[l],
                                   t_p, t_s)

    btm = BIG_TOKEN_TILE
    gate_tn = 2 * bw
    n_gate_steps = 2 * D_MODEL // gate_tn
    gate_grid = (n_gate_steps, m // btm)
    gates, w_attn_o_bf, w_conv_o_bf, w_out_bf = _wres_matmul(
        [(h, w_gates_bf, gate_tn, [lambda j: j], None)],
        [],
        [(sds((m, 2 * D_MODEL), BF16), (btm, gate_tn), lambda j, i: (i, j))],
        _epi_sigmoid, n_col_steps=n_gate_steps, name="proj_gates", tm=btm,
        side_casts=[_slab_stream(w_attn_o[l], (HEAD_DIM, D_MODEL), *gate_grid, row_of_slab=_interleaved_head),
                    _slab_stream(w_conv_o[l], (CONV_DIM // 16, D_MODEL // 4), *gate_grid),
                    _slab_stream(w_out[l], (D_MODEL // 16, D_MODEL // 4), *gate_grid)])

    n_gate_blk = D_MODEL // (2 * bw)
    one_block = [lambda j: j]
    ffn_slab = (D_MODEL // 32, D_FF // 2)
    mix, w_fg_bf = _wres_matmul(
        [(Stacked(o_p, o_s), w_attn_o_bf, 2 * bw, one_block, None),
         (Stacked(c_p, c_s), w_conv_o_bf, 2 * bw, one_block, None)],
        [(gates, (tm, 2 * bw), lambda j, i: (i, j)),
         (gates, (tm, 2 * bw), lambda j, i: (i, j + n_gate_blk))],
        [(sds((m, D_MODEL), BF16), (tm, 2 * bw), lambda j, i: (i, j))],
        _epi_merge, n_col_steps=n_gate_blk, name="merge",
        side_casts=[_slab_stream(w_ffn_in[l], ffn_slab, n_gate_blk, m // tm, col0=0, n_cols=D_FF)])

    x1, w_fu_bf = _wres_matmul(
        [(mix, w_out_bf, 2 * bw, one_block, None)],
        [(x, (tm, 2 * bw), lambda j, i: (i, j))],
        [(sds((m, D_MODEL), F32), (tm, 2 * bw), lambda j, i: (i, j))],
        _epi_residual, n_col_steps=D_MODEL // (2 * bw), name="out_proj",
        side_casts=[_slab_stream(w_ffn_in[l], ffn_slab, D_MODEL // (2 * bw), m // tm, col0=D_FF, n_cols=D_FF)])

    h2 = _rmsnorm(x1, g_ffn_norm[l], BF16, "rmsnorm_ffn")
    ftm = FFN_TOKEN_TILE
    n_ff_blk = D_FF // FFN_BLOCK
    n_ff_steps = n_ff_blk * (m // ftm)
    act, w_ffn_out_bf = _wres_matmul(
        [(h2, (w_fg_bf, w_fu_bf), FFN_BLOCK, [lambda j: j, lambda j: j], None)],
        [],
        [(sds((m, D_FF), BF16), (ftm, FFN_BLOCK), lambda j, i: (i, j))],
        _epi_swiglu, n_col_steps=n_ff_blk, name="ffn_in", tm=ftm,
        side_casts=[_slab_stream(w_ffn_out[l], (D_FF // n_ff_steps, D_MODEL), n_ff_blk, m // ftm)])

    x2 = _ares_matmul_residual(act, w_ffn_out_bf, x1, tm=tm, tn=512, name="ffn_out")
    norm_tile = 256
    y_p = _rmsnorm(x2, g_final, F32, "rmsnorm_final_prompt", row_tile0=0, n_tiles=t_p // norm_tile,
                   tm=norm_tile)
    y_s = _rmsnorm(x2, g_final, F32, "rmsnorm_final_sample", row_tile0=t_p // norm_tile,
                   n_tiles=(m - t_p) // norm_tile, tm=norm_tile)

    y_prompt = y_p.reshape(x_prompt.shape)
    y_sample = y_s.reshape(x_sample.shape)
    k_prompt = k_all[t_p - WINDOW:t_p].reshape(1, 1, WINDOW, N_KV_HEADS, HEAD_DIM)
    v_prompt = v_all[t_p - WINDOW:t_p].reshape(1, 1, WINDOW, N_KV_HEADS, HEAD_DIM)
    conv_prompt = cu[t_p - (CONV_WIDTH - 1):t_p].reshape(1, 1, CONV_WIDTH - 1, CONV_DIM)
    k_sample = k_win.reshape(1, b_s, WINDOW, N_KV_HEADS, HEAD_DIM)
    v_sample = v_win.reshape(1, b_s, WINDOW, N_KV_HEADS, HEAD_DIM)
    conv_sample = conv_state.reshape(1, b_s, CONV_WIDTH - 1, CONV_DIM)
    return (y_prompt, y_sample, k_prompt, v_prompt, conv_prompt, k_sample, v_sample, conv_sample)
```

```python
import collections
import functools

import jax
import jax.numpy as jnp
from jax import lax
from jax.experimental import pallas as pl
from jax.experimental.pallas import tpu as pltpu

D_MODEL = 4096
HEAD_DIM = 64
N_HEADS = 32
N_KV_HEADS = 4
N_GROUP = N_HEADS // N_KV_HEADS
WINDOW = 128
ROPE_THETA = 10000.0
CONV_DIM = 2048
CONV_WIDTH = 31
D_FF = 11008
PAST_LEN = 8192
Q_DIM = N_HEADS * HEAD_DIM
KV_DIM = N_KV_HEADS * HEAD_DIM
EPS = 1e-6
NEG = -1e30

V7X_LANES = 128
V7X_VMEM_LIMIT_BYTES = 60000 * 1024

F32 = jnp.float32
BF16 = jnp.bfloat16

TOKEN_TILE = 512
WEIGHT_BLOCK = 512
CAST_ROWS = 256
N_KV_PAIRS = N_KV_HEADS // 2
CHUNKS_PER_PAIR = N_GROUP

Stacked = collections.namedtuple("Stacked", ["prompt", "sample"])

SideCast = collections.namedtuple("SideCast", ["array", "slab", "in_map", "out_shape", "out_map"])


def _slab_stream(array, slab, n_outer, n_inner, *, col0=0, n_cols=None, row_of_slab=None):
    k, n = array.shape
    n_cols = n - col0 if n_cols is None else n_cols
    assert k % slab[0] == 0 and n_cols % slab[1] == 0 and col0 % slab[1] == 0
    n_r, n_c = k // slab[0], n_cols // slab[1]
    assert n_r * n_c <= n_outer * n_inner, "not enough grid steps to cast every slab"

    def rc(j, i):
        t = jnp.minimum(j * n_inner + i, n_r * n_c - 1)
        return t // n_c, t % n_c

    def in_map(j, i):
        r, c = rc(j, i)
        return (r if row_of_slab is None else row_of_slab(r), col0 // slab[1] + c)

    return SideCast(array, slab, in_map, (k, n_cols), rc)


def _params(*sem):
    return pltpu.CompilerParams(dimension_semantics=sem,
                                vmem_limit_bytes=V7X_VMEM_LIMIT_BYTES)


def _stacked_specs(op, block, imap, tm):
    if not isinstance(op, Stacked):
        return [op], [pl.BlockSpec(block, imap)]
    n_p = op.prompt.shape[0] // tm
    assert op.prompt.shape[0] % tm == 0 and op.sample.shape[0] % tm == 0

    def p_map(*ids):
        r, c = imap(*ids)
        return (jnp.minimum(r, n_p - 1), c)

    def s_map(*ids):
        r, c = imap(*ids)
        return (jnp.maximum(r - n_p, 0), c)

    return [op.prompt, op.sample], [pl.BlockSpec(block, p_map), pl.BlockSpec(block, s_map)]


def _n_prompt_tiles(op, tm):
    return op.prompt.shape[0] // tm if isinstance(op, Stacked) else None


def _load(refs, n_p, row_tile):
    if n_p is None:
        return refs[0][...]
    return jnp.where(row_tile < n_p, refs[0][...], refs[1][...])


def _rmsnorm_kernel(*refs, n_p):
    x_refs, (g_ref, o_ref) = refs[:-2], refs[-2:]
    x = _load(x_refs, n_p, pl.program_id(0))
    y = x * lax.rsqrt(jnp.mean(x * x, axis=-1, keepdims=True) + EPS) * g_ref[...]
    o_ref[...] = y.astype(o_ref.dtype)


def _rmsnorm(x, g, out_dtype, name, *, row_tile0=0, n_tiles=None, tm=256):
    d = g.shape[0]
    rows = (x.prompt.shape[0] + x.sample.shape[0]) if isinstance(x, Stacked) else x.shape[0]
    n_tiles = rows // tm if n_tiles is None else n_tiles
    arrays, specs = _stacked_specs(x, (tm, d), lambda i: (i + row_tile0, 0), tm)
    return pl.pallas_call(
        functools.partial(_rmsnorm_kernel, n_p=_n_prompt_tiles(x, tm)),
        grid=(n_tiles,),
        in_specs=specs + [pl.BlockSpec((1, d), lambda i: (0, 0))],
        out_specs=pl.BlockSpec((tm, d), lambda i: (i, 0)),
        out_shape=jax.ShapeDtypeStruct((n_tiles * tm, d), out_dtype),
        compiler_params=_params("arbitrary"),
        name=name,
    )(*arrays, g.reshape(1, d))


def _cast_plain(w_refs, wbf_ref):
    k, bw = w_refs[0].shape

    def body(c, carry):
        r = pl.multiple_of(c * CAST_ROWS, CAST_ROWS)
        for b, w_ref in enumerate(w_refs):
            wbf_ref[pl.ds(r, CAST_ROWS), b * bw:(b + 1) * bw] = w_ref[pl.ds(r, CAST_ROWS), :].astype(BF16)
        return carry

    lax.fori_loop(0, k // CAST_ROWS, body, 0)


def _cast_interleave_heads(w_refs, wbf_ref):
    w_even, w_odd = w_refs
    k, bw = w_even.shape

    def body(c, carry):
        r = pl.multiple_of(c * CAST_ROWS, CAST_ROWS)
        a = w_even[pl.ds(r, CAST_ROWS), :]
        b = w_odd[pl.ds(r, CAST_ROWS), :]
        pieces = []
        for g in range(bw // HEAD_DIM):
            pieces += [a[:, g * HEAD_DIM:(g + 1) * HEAD_DIM], b[:, g * HEAD_DIM:(g + 1) * HEAD_DIM]]
        wbf_ref[pl.ds(r, CAST_ROWS), :] = jnp.concatenate(pieces, axis=1).astype(BF16)
        return carry

    lax.fori_loop(0, k // CAST_ROWS, body, 0)


def _interleaved_head(slot):
    chunk, parity = divmod(slot, 2)
    pair, g = divmod(chunk, CHUNKS_PER_PAIR)
    return (2 * pair + parity) * N_GROUP + g


def _wres_kernel(*refs, layout, casts, epilogue):
    a_counts, w_counts, extra_counts, n_out, n_side, a_np, extra_np = layout
    pos = 0

    def take(counts):
        nonlocal pos
        groups = []
        for n in counts:
            groups.append(refs[pos:pos + n])
            pos += n
        return groups

    a_groups = take(a_counts)
    w_groups = take(w_counts)
    extra_groups = take(extra_counts)
    side_in = refs[pos:pos + n_side]
    pos += n_side
    out_refs = refs[pos:pos + n_out]
    pos += n_out
    side_out = refs[pos:pos + n_side]
    scratch_refs = list(refs[pos + n_side:])
    wbf_refs = [scratch_refs.pop(0) if cast is not None else None for cast in casts]
    i = pl.program_id(1)

    if any(cast is not None for cast in casts):
        @pl.when(i == 0)
        def _():
            for cast, w_refs, wbf_ref in zip(casts, w_groups, wbf_refs):
                if cast is not None:
                    cast(w_refs, wbf_ref)

    for s_in, s_out in zip(side_in, side_out):
        s_out[...] = s_in[...].astype(s_out.dtype)

    accs = []
    for a_refs, n_p, w_refs, wbf_ref in zip(a_groups, a_np, w_groups, wbf_refs):
        a = _load(a_refs, n_p, i)
        if wbf_ref is not None:
            accs.append(jnp.dot(a, wbf_ref[...], preferred_element_type=F32))
        else:
            parts = [jnp.dot(a, w_ref[...], preferred_element_type=F32) for w_ref in w_refs]
            accs.append(parts[0] if len(parts) == 1 else jnp.concatenate(parts, axis=1))
    extras = [_load(e_refs, n_p, i) for e_refs, n_p in zip(extra_groups, extra_np)]
    for o_ref, o in zip(out_refs, epilogue(accs, extras)):
        o_ref[...] = o.astype(o_ref.dtype)


def _wres_matmul(terms, extras, outs, epilogue, *, n_col_steps, name, tm=TOKEN_TILE,
                 weight_buffers=2, side_casts=()):
    a0 = terms[0][0]
    m = (a0.prompt.shape[0] + a0.sample.shape[0]) if isinstance(a0, Stacked) else a0.shape[0]
    assert m % tm == 0
    in_arrays, in_specs, scratch, casts = [], [], [], []
    a_counts, w_counts, extra_counts, a_np, extra_np = [], [], [], [], []
    for a, w, bw, col_fns, cast in terms:
        k = (w[0] if isinstance(w, tuple) else w).shape[0]
        arrs, specs = _stacked_specs(a, (tm, k), lambda j, i: (i, 0), tm)
        in_arrays += arrs
        in_specs += specs
        a_counts.append(len(arrs))
        a_np.append(_n_prompt_tiles(a, tm))
    for a, w, bw, col_fns, cast in terms:
        w_of_block = w if isinstance(w, tuple) else (w,) * len(col_fns)
        k = w_of_block[0].shape[0]
        for fn, w_blk in zip(col_fns, w_of_block):
            in_arrays.append(w_blk)
            in_specs.append(pl.BlockSpec((k, bw), functools.partial(lambda j, i, fn: (0, fn(j)), fn=fn),
                                         pipeline_mode=pl.Buffered(weight_buffers)))
        w_counts.append(len(col_fns))
        casts.append(cast)
        if cast is not None:
            scratch.append(pltpu.VMEM((k, bw * len(col_fns)), BF16))
    for op, blk, imap in extras:
        arrs, specs = _stacked_specs(op, blk, imap, tm)
        in_arrays += arrs
        in_specs += specs
        extra_counts.append(len(arrs))
        extra_np.append(_n_prompt_tiles(op, tm))
    for sc in side_casts:
        in_arrays.append(sc.array)
        in_specs.append(pl.BlockSpec(sc.slab, sc.in_map))
    layout = (tuple(a_counts), tuple(w_counts), tuple(extra_counts), len(outs), len(side_casts),
              tuple(a_np), tuple(extra_np))
    return pl.pallas_call(
        functools.partial(_wres_kernel, layout=layout, casts=tuple(casts), epilogue=epilogue),
        grid=(n_col_steps, m // tm),
        in_specs=in_specs,
        out_specs=([pl.BlockSpec(blk, imap) for _, blk, imap in outs]
                   + [pl.BlockSpec(sc.slab, sc.out_map) for sc in side_casts]),
        out_shape=([s for s, _, _ in outs]
                   + [jax.ShapeDtypeStruct(sc.out_shape, BF16) for sc in side_casts]),
        scratch_shapes=scratch,
        compiler_params=_params("arbitrary", "arbitrary"),
        name=name,
    )(*in_arrays)


def _ares_kernel(a_ref, w_ref, x_ref, o_ref):
    acc = jnp.dot(a_ref[...], w_ref[...], preferred_element_type=F32)
    o_ref[...] = x_ref[...] + acc


def _ares_matmul_residual(a, w_bf, x, *, tm, tn, name):
    m, k = a.shape
    n = w_bf.shape[1]
    return pl.pallas_call(
        _ares_kernel,
        grid=(m // tm, n // tn),
        in_specs=[pl.BlockSpec((tm, k), lambda i, j: (i, 0)),
                  pl.BlockSpec((k, tn), lambda i, j: (0, j)),
                  pl.BlockSpec((tm, tn), lambda i, j: (i, j))],
        out_specs=pl.BlockSpec((tm, tn), lambda i, j: (i, j)),
        out_shape=jax.ShapeDtypeStruct((m, n), F32),
        compiler_params=_params("arbitrary", "arbitrary"),
        name=name,
    )(a, w_bf, x)


def _swiglu_matmul(h, w_gate, w_up, w_next, *, tm, bw, name):
    m, k = h.shape
    n_blk, n_i = w_gate.shape[1] // bw, m // tm
    slab = w_next.shape[0] // (n_blk * n_i)
    assert slab * n_blk * n_i == w_next.shape[0] and slab % 16 == 0

    def body(a_ref, wg_ref, wu_ref, s_ref, o_ref, so_ref):
        so_ref[...] = s_ref[...].astype(BF16)
        a = a_ref[...]
        g = jnp.dot(a, wg_ref[...], preferred_element_type=F32)
        u = jnp.dot(a, wu_ref[...], preferred_element_type=F32)
        o_ref[...] = (jax.nn.silu(g) * u).astype(o_ref.dtype)

    slab_map = lambda j, i: (j * n_i + i, 0)

    def outer(h_ref, wg_ref, wu_ref, s_ref, o_ref, so_ref):
        pltpu.emit_pipeline(
            body, grid=(n_blk, n_i),
            in_specs=[pl.BlockSpec((tm, k), lambda j, i: (i, 0)),
                      pl.BlockSpec((k, bw), lambda j, i: (0, j)),
                      pl.BlockSpec((k, bw), lambda j, i: (0, j)),
                      pl.BlockSpec((slab, w_next.shape[1]), slab_map)],
            out_specs=[pl.BlockSpec((tm, bw), lambda j, i: (i, j)),
                       pl.BlockSpec((slab, w_next.shape[1]), slab_map)],
        )(h_ref, wg_ref, wu_ref, s_ref, o_ref, so_ref)

    any_spec = pl.BlockSpec(memory_space=pl.ANY)
    return pl.pallas_call(
        outer,
        in_specs=[any_spec] * 4,
        out_specs=[any_spec] * 2,
        out_shape=[jax.ShapeDtypeStruct((m, w_gate.shape[1]), BF16),
                   jax.ShapeDtypeStruct(w_next.shape, BF16)],
        compiler_params=pltpu.CompilerParams(vmem_limit_bytes=V7X_VMEM_LIMIT_BYTES),
        name=name,
    )(h, w_gate, w_up, w_next)


def _rope(x, cos128, sin128):
    n = x.shape[1]
    reps = n // V7X_LANES
    c = jnp.concatenate([cos128] * reps, axis=1) if reps > 1 else cos128
    s = jnp.concatenate([sin128] * reps, axis=1) if reps > 1 else sin128
    lane = lax.broadcasted_iota(jnp.int32, x.shape, 1)
    first_half = (lane & (HEAD_DIM - 1)) < (HEAD_DIM // 2)
    partner = jnp.where(first_half,
                        pltpu.roll(x, n - HEAD_DIM // 2, axis=1),
                        pltpu.roll(x, HEAD_DIM // 2, axis=1))
    return x * c + partner * s


def _epi_q(accs, extras):
    cos, sin = extras
    return [_rope(accs[0], cos, sin)]


def _epi_kv(accs, extras):
    cos, sin = extras
    acc = accs[0]
    return [_rope(acc[:, :KV_DIM], cos, sin), acc[:, KV_DIM:]]


def _epi_glu(accs, extras):
    acc = accs[0]
    half = acc.shape[1] // 2
    return [acc[:, :half] * jax.nn.sigmoid(acc[:, half:])]


def _epi_sigmoid(accs, extras):
    return [jax.nn.sigmoid(accs[0])]


def _epi_merge(accs, extras):
    g_a, g_c = extras
    return [g_a.astype(F32) * accs[0] + g_c.astype(F32) * accs[1]]


def _epi_residual(accs, extras):
    return [extras[0] + accs[0]]


def _attend(qs, kx, vx_ones, valid, sk, n_rep):
    r, nk = qs.shape[0], kx.shape[0]
    s = lax.dot_general(qs, kx, (((1,), (1,)), ((), ())), preferred_element_type=F32)
    s = jnp.where(valid[None], s.reshape(n_rep, r // n_rep, nk), NEG).reshape(r, nk)
    m = jnp.maximum(jnp.broadcast_to(jnp.max(s, axis=-1, keepdims=True), sk.shape), sk)
    p = jnp.exp(s - jnp.concatenate([m] * (nk // V7X_LANES), axis=1)).astype(BF16)
    ol = jnp.dot(p, vx_ones, preferred_element_type=F32)
    return ol[:, :V7X_LANES] / (ol[:, V7X_LANES:] + jnp.exp(sk - m))


def _attend_all_heads(q_ref, o_ref, sink_ref, k_of_pair, v_of_pair, valid):
    rows = q_ref.shape[0]
    nk = valid.shape[1]
    lane = lax.broadcasted_iota(jnp.int32, (nk, V7X_LANES), 1)
    ones = jnp.ones((nk, V7X_LANES), BF16)
    scale = HEAD_DIM ** -0.5
    for pair in range(N_KV_PAIRS):
        kcol = k_of_pair(pair) * scale
        vcol = v_of_pair(pair)
        c0 = pair * CHUNKS_PER_PAIR
        qs = jnp.concatenate([q_ref[:, (c0 + g) * V7X_LANES:(c0 + g + 1) * V7X_LANES]
                              for g in range(CHUNKS_PER_PAIR)], axis=0)
        o_pair = None
        for parity in range(2):
            half = (lane < HEAD_DIM) if parity == 0 else (lane >= HEAD_DIM)
            kx = jnp.where(half, kcol, 0.0).astype(BF16)
            vx = jnp.concatenate([jnp.where(half, vcol, 0.0).astype(BF16), ones], axis=1)
            o = _attend(qs, kx, vx, valid, sink_ref[2 * pair + parity], CHUNKS_PER_PAIR)
            o_pair = o if o_pair is None else o_pair + o
        for g in range(CHUNKS_PER_PAIR):
            o_ref[:, (c0 + g) * V7X_LANES:(c0 + g + 1) * V7X_LANES] = (
                o_pair[g * rows:(g + 1) * rows].astype(o_ref.dtype))


def _sink_table(sinks, rows):
    per_row = jnp.repeat(sinks.astype(F32).reshape(N_KV_HEADS, N_GROUP), rows, axis=1)
    return jnp.broadcast_to(per_row[:, :, None], (N_KV_HEADS, N_GROUP * rows, V7X_LANES))


def _attn_prompt_kernel(sink_ref, q_ref, kc_ref, kp_ref, vc_ref, vp_ref, o_ref):
    n = pl.program_id(0)
    blk = q_ref.shape[0]
    row = lax.broadcasted_iota(jnp.int32, (blk, 2 * blk), 0)
    col = lax.broadcasted_iota(jnp.int32, (blk, 2 * blk), 1)
    diff = row + blk - col
    valid = (diff >= 0) & (diff < WINDOW) & ((n > 0) | (col >= blk))

    def both_blocks(prev_ref, cur_ref):
        return lambda pair: jnp.concatenate(
            [prev_ref[:, pair * V7X_LANES:(pair + 1) * V7X_LANES],
             cur_ref[:, pair * V7X_LANES:(pair + 1) * V7X_LANES]], axis=0)

    _attend_all_heads(q_ref, o_ref, sink_ref, both_blocks(kp_ref, kc_ref), both_blocks(vp_ref, vc_ref), valid)


def _attn_prompt(q, k, v, sinks, t_prompt):
    blk = WINDOW
    nb = t_prompt // blk
    cur = lambda n: (n, 0)
    prev = lambda n: (jnp.maximum(n - 1, 0), 0)
    return pl.pallas_call(
        _attn_prompt_kernel,
        grid=(nb,),
        in_specs=[pl.BlockSpec((N_KV_HEADS, N_GROUP * blk, V7X_LANES), lambda n: (0, 0, 0)),
                  pl.BlockSpec((blk, Q_DIM), cur),
                  pl.BlockSpec((blk, KV_DIM), cur),
                  pl.BlockSpec((blk, KV_DIM), prev),
                  pl.BlockSpec((blk, KV_DIM), cur),
                  pl.BlockSpec((blk, KV_DIM), prev)],
        out_specs=pl.BlockSpec((blk, Q_DIM), cur),
        out_shape=jax.ShapeDtypeStruct((t_prompt, Q_DIM), BF16),
        compiler_params=_params("arbitrary"),
        name="attn_prompt",
    )(_sink_table(sinks, blk), q, k, k, v, v)


SAMPLE_BATCH_BLOCK = 8
NEW_KEY_PAD = 128


def _attn_sample_kernel(sink_ref, q_ref, kn_ref, vn_ref, ck_ref, cv_ref, o_ref, ko_ref, vo_ref):
    bb, n_past = ck_ref.shape[0], ck_ref.shape[1]
    rows = q_ref.shape[0]
    t_new = rows // bb
    n_cache = bb * n_past
    nk = n_cache + NEW_KEY_PAD
    lg_t, lg_p = t_new.bit_length() - 1, n_past.bit_length() - 1

    for b in range(bb):
        ko_ref[b, 0:n_past - t_new, :] = ck_ref[b, t_new:n_past, :]
        ko_ref[b, n_past - t_new:n_past, :] = kn_ref[b * t_new:(b + 1) * t_new, :]
        vo_ref[b, 0:n_past - t_new, :] = cv_ref[b, t_new:n_past, :]
        vo_ref[b, n_past - t_new:n_past, :] = vn_ref[b * t_new:(b + 1) * t_new, :]

    r = lax.broadcasted_iota(jnp.int32, (rows, nk), 0)
    c = lax.broadcasted_iota(jnp.int32, (rows, nk), 1)
    b_r, t = r >> lg_t, r & (t_new - 1)
    d_cache = t + n_past - (c & (n_past - 1))
    valid_cache = (c < n_cache) & ((c >> lg_p) == b_r) & (d_cache >= 0) & (d_cache < WINDOW)
    cn = c - n_cache
    d_new = t - (cn & (t_new - 1))
    valid_new = (cn >= 0) & (cn < rows) & ((cn >> lg_t) == b_r) & (d_new >= 0) & (d_new < WINDOW)
    valid = valid_cache | valid_new

    pad = jnp.zeros((NEW_KEY_PAD - rows, V7X_LANES), F32)

    def keys(cache_ref, new_ref):
        def of_pair(pair):
            ls = slice(pair * V7X_LANES, (pair + 1) * V7X_LANES)
            return jnp.concatenate([cache_ref[:, :, ls].reshape(n_cache, V7X_LANES), new_ref[:, ls], pad], axis=0)
        return of_pair

    _attend_all_heads(q_ref, o_ref, sink_ref, keys(ck_ref, kn_ref), keys(cv_ref, vn_ref), valid)


def _attn_sample(q, k, v, cache_k, cache_v, sinks, row0, t_new):
    b, n_past, _ = cache_k.shape
    assert n_past == WINDOW and t_new & (t_new - 1) == 0
    bb = SAMPLE_BATCH_BLOCK
    rows = bb * t_new
    assert row0 % rows == 0 and rows <= NEW_KEY_PAD
    tok = lambda i: (row0 // rows + i, 0)
    i3 = lambda i: (i, 0, 0)
    return pl.pallas_call(
        _attn_sample_kernel,
        grid=(b // bb,),
        in_specs=[pl.BlockSpec((N_KV_HEADS, N_GROUP * rows, V7X_LANES), lambda i: (0, 0, 0)),
                  pl.BlockSpec((rows, Q_DIM), tok),
                  pl.BlockSpec((rows, KV_DIM), tok),
                  pl.BlockSpec((rows, KV_DIM), tok),
                  pl.BlockSpec((bb, n_past, KV_DIM), i3),
                  pl.BlockSpec((bb, n_past, KV_DIM), i3)],
        out_specs=[pl.BlockSpec((rows, Q_DIM), lambda i: (i, 0)),
                   pl.BlockSpec((bb, n_past, KV_DIM), i3),
                   pl.BlockSpec((bb, n_past, KV_DIM), i3)],
        out_shape=[jax.ShapeDtypeStruct((b * t_new, Q_DIM), BF16),
                   jax.ShapeDtypeStruct((b, n_past, KV_DIM), F32),
                   jax.ShapeDtypeStruct((b, n_past, KV_DIM), F32)],
        compiler_params=_params("arbitrary"),
        name="attn_sample",
    )(_sink_table(sinks, rows), q, k, v, cache_k, cache_v)


def _ln_swish(c, g, b):
    mu = jnp.mean(c, axis=-1, keepdims=True)
    var = jnp.mean(jnp.square(c - mu), axis=-1, keepdims=True)
    y = (c - mu) * lax.rsqrt(var + EPS) * g + b
    return jax.nn.silu(y)


CONV_HALO = 32
SUBLANES = 8
CONV_ROW_TILE = 128


def _conv_prompt_kernel(cur_ref, halo_ref, w_ref, b_ref, g_ref, bl_ref, side_ref, o_ref, side_o_ref,
                        xs_ref, c_ref):
    i = pl.program_id(0)
    tt = cur_ref.shape[0]
    side_o_ref[...] = side_ref[...].astype(side_o_ref.dtype)
    xs_ref[0:CONV_HALO, :] = jnp.where(i > 0, halo_ref[...], 0.0)
    xs_ref[CONV_HALO:CONV_HALO + tt, :] = cur_ref[...]
    off = CONV_HALO - (CONV_WIDTH - 1)
    n_groups = CONV_ROW_TILE // SUBLANES
    taps_by_shift = [[(w, (off + w) // SUBLANES) for w in range(CONV_WIDTH) if (off + w) % SUBLANES == s]
                     for s in range(SUBLANES)]
    n_in = max(n_groups + max(d for _, d in taps) + (s > 0) for s, taps in enumerate(taps_by_shift))
    assert tt - CONV_ROW_TILE + n_in * SUBLANES <= CONV_HALO + tt
    sub = lax.broadcasted_iota(jnp.int32, (SUBLANES, V7X_LANES), 0)

    def lane_chunk(cc, carry):
        ls = pl.ds(pl.multiple_of(cc * V7X_LANES, V7X_LANES), V7X_LANES)
        for r0 in range(0, tt, CONV_ROW_TILE):
            x0 = [xs_ref[pl.ds(r0 + kg * SUBLANES, SUBLANES), ls] for kg in range(n_in)]
            acc = [None] * n_groups
            for s, taps in enumerate(taps_by_shift):
                wv = {w: jnp.broadcast_to(w_ref[w:w + 1, ls], (SUBLANES, V7X_LANES)) for w, _ in taps}
                n_used = n_groups + max(d for _, d in taps)
                if s == 0:
                    xsft = x0
                else:
                    rot = [pltpu.roll(x0[kg], SUBLANES - s, axis=0) for kg in range(n_used + 1)]
                    xsft = [jnp.where(sub < SUBLANES - s, rot[kg], rot[kg + 1]) for kg in range(n_used)]
                for kg in range(n_used):
                    for w, d in taps:
                        og = kg - d
                        if 0 <= og < n_groups:
                            term = xsft[kg] * wv[w]
                            acc[og] = term if acc[og] is None else acc[og] + term
            bias = b_ref[:, ls]
            for og in range(n_groups):
                c_ref[pl.ds(r0 + og * SUBLANES, SUBLANES), ls] = acc[og] + bias
        return carry

    lax.fori_loop(0, cur_ref.shape[1] // V7X_LANES, lane_chunk, 0)
    o_ref[...] = _ln_swish(c_ref[...], g_ref[...], bl_ref[...]).astype(o_ref.dtype)


def _conv_prompt(cu, w_dw, b_dw, g_ln, b_ln, t_prompt, make_side_cast):
    tt = 256
    c = cu.shape[1]
    per = tt // CONV_HALO
    row1 = lambda i: (0, 0)
    side = make_side_cast(t_prompt // tt)
    return pl.pallas_call(
        _conv_prompt_kernel,
        grid=(t_prompt // tt,),
        in_specs=[pl.BlockSpec((tt, c), lambda i: (i, 0)),
                  pl.BlockSpec((CONV_HALO, c), lambda i: (jnp.maximum(i * per - 1, 0), 0)),
                  pl.BlockSpec((CONV_WIDTH, c), row1),
                  pl.BlockSpec((1, c), row1),
                  pl.BlockSpec((1, c), row1),
                  pl.BlockSpec((1, c), row1),
                  pl.BlockSpec(side.slab, lambda i: side.in_map(0, i))],
        out_specs=[pl.BlockSpec((tt, c), lambda i: (i, 0)),
                   pl.BlockSpec(side.slab, lambda i: side.out_map(0, i))],
        out_shape=[jax.ShapeDtypeStruct((t_prompt, c), BF16),
                   jax.ShapeDtypeStruct(side.out_shape, BF16)],
        scratch_shapes=[pltpu.VMEM((CONV_HALO + tt, c), F32), pltpu.VMEM((tt, c), F32)],
        compiler_params=_params("arbitrary"),
        name="conv_prompt",
    )(cu, cu, w_dw, b_dw.reshape(1, c), g_ln.reshape(1, c), b_ln.reshape(1, c), side.array)


CONV_SAMPLE_BATCH_TILE = 32


def _conv_sample_kernel(st_ref, cu_ref, w_ref, b_ref, g_ref, bl_ref, o_ref, so_ref, c_ref):
    n_hist, bt, c = st_ref.shape
    t_new = cu_ref.shape[0]

    def lane_chunk(cc, carry):
        ls = pl.ds(pl.multiple_of(cc * V7X_LANES, V7X_LANES), V7X_LANES)
        bias = b_ref[:, ls]
        for r0 in range(0, bt, SUBLANES):
            rs = slice(r0, r0 + SUBLANES)
            xs = [st_ref[r, rs, ls] for r in range(n_hist)] + [cu_ref[t, rs, ls] for t in range(t_new)]
            for t in range(t_new):
                acc = xs[t] * w_ref[0:1, ls]
                for w in range(1, CONV_WIDTH):
                    acc = acc + xs[t + w] * w_ref[w:w + 1, ls]
                c_ref[t, rs, ls] = acc + bias
            for r in range(n_hist):
                so_ref[r, rs, ls] = xs[r + t_new]
        return carry

    lax.fori_loop(0, c // V7X_LANES, lane_chunk, 0)
    for t in range(t_new):
        o_ref[t] = _ln_swish(c_ref[t], g_ref[...], bl_ref[...]).astype(o_ref.dtype)


def _conv_sample(cu_t, state_t, w_dw, b_dw, g_ln, b_ln):
    n_hist, b, c = state_t.shape
    t_new = cu_t.shape[0]
    assert n_hist == CONV_WIDTH - 1
    bt = CONV_SAMPLE_BATCH_TILE
    blk = lambda i: (0, i, 0)
    row1 = lambda i: (0, 0)
    return pl.pallas_call(
        _conv_sample_kernel,
        grid=(b // bt,),
        in_specs=[pl.BlockSpec((n_hist, bt, c), blk),
                  pl.BlockSpec((t_new, bt, c), blk),
                  pl.BlockSpec((CONV_WIDTH, c), row1),
                  pl.BlockSpec((1, c), row1),
                  pl.BlockSpec((1, c), row1),
                  pl.BlockSpec((1, c), row1)],
        out_specs=[pl.BlockSpec((t_new, bt, c), blk),
                   pl.BlockSpec((n_hist, bt, c), blk)],
        out_shape=[jax.ShapeDtypeStruct((t_new, b, c), BF16),
                   jax.ShapeDtypeStruct((n_hist, b, c), F32)],
        scratch_shapes=[pltpu.VMEM((t_new, bt, c), F32)],
        compiler_params=_params("arbitrary"),
        name="conv_sample",
    )(state_t, cu_t, w_dw, b_dw.reshape(1, c), g_ln.reshape(1, c), b_ln.reshape(1, c))


def _rope_tables(pos):
    half = HEAD_DIM // 2
    inv_freq = ROPE_THETA ** (-jnp.arange(half, dtype=F32) / half)
    ang = pos.astype(F32)[:, None] * inv_freq[None, :]
    cos, sin = jnp.cos(ang), jnp.sin(ang)
    cos64 = jnp.concatenate([cos, cos], axis=-1)
    sin64 = jnp.concatenate([-sin, sin], axis=-1)
    reps = V7X_LANES // HEAD_DIM
    return jnp.tile(cos64, (1, reps)), jnp.tile(sin64, (1, reps))


BIG_TOKEN_TILE = 1088
FFN_TOKEN_TILE = 2176
FFN_BLOCK = 256


def kernel(x_prompt, x_sample, cache_k, cache_v, state_conv, g_mix_norm, w_in, sinks, w_attn_o,
           w_dw, b_dw, g_conv_ln, b_conv_ln, w_conv_o, w_out, g_ffn_norm, w_ffn_in, w_ffn_out,
           g_final):
    depth = w_in.shape[0]
    assert depth == 1, "single-layer trunk"
    t_p = x_prompt.shape[1]
    b_s, t_s, _ = x_sample.shape
    m = t_p + b_s * t_s
    tm = TOKEN_TILE
    bw = WEIGHT_BLOCK
    sds = jax.ShapeDtypeStruct
    plain = _cast_plain

    x = Stacked(x_prompt.reshape(t_p, D_MODEL), x_sample.reshape(b_s * t_s, D_MODEL))
    pos = jnp.concatenate([jnp.arange(t_p, dtype=jnp.int32),
                           jnp.tile(PAST_LEN + jnp.arange(t_s, dtype=jnp.int32), b_s)])
    cos_t, sin_t = _rope_tables(pos)
    rope_extras = [(cos_t, (tm, V7X_LANES), lambda j, i: (i, 0)),
                   (sin_t, (tm, V7X_LANES), lambda j, i: (i, 0))]

    l = 0
    w_in_l = w_in[l]
    h = _rmsnorm(x, g_mix_norm[l], BF16, "rmsnorm_mix")

    kv_blk0 = Q_DIM // bw
    ua_blk0 = (Q_DIM + 2 * KV_DIM) // bw
    ub_blk0 = ua_blk0 + CONV_DIM // bw
    ga_blk0 = ub_blk0 + CONV_DIM // bw
    assert 2 * KV_DIM == bw and N_GROUP * HEAD_DIM == bw

    (q,) = _wres_matmul(
        [(h, w_in_l, bw, [lambda j: 2 * j, lambda j: 2 * j + 1], _cast_interleave_heads)],
        rope_extras,
        [(sds((m, Q_DIM), BF16), (tm, 2 * bw), lambda j, i: (i, j))],
        _epi_q, n_col_steps=N_KV_PAIRS, name="proj_q")

    k_all, v_all = _wres_matmul(
        [(h, w_in_l, bw, [lambda j: kv_blk0], plain)],
        rope_extras,
        [(sds((m, KV_DIM), F32), (tm, KV_DIM), lambda j, i: (i, 0)),
         (sds((m, KV_DIM), F32), (tm, KV_DIM), lambda j, i: (i, 0))],
        _epi_kv, n_col_steps=1, name="proj_kv")

    (cu,) = _wres_matmul(
        [(h, w_in_l, bw, [lambda j: ua_blk0 + j, lambda j: ub_blk0 + j], plain)],
        [],
        [(sds((m, CONV_DIM), F32), (tm, bw), lambda j, i: (i, j))],
        _epi_glu, n_col_steps=CONV_DIM // bw, name="proj_glu")

    n_past = cache_k.shape[2]
    o_p = _attn_prompt(q, k_all, v_all, sinks[l], t_p)
    o_s, k_win, v_win = _attn_sample(q, k_all, v_all,
                                     cache_k[l].reshape(b_s, n_past, KV_DIM),
                                     cache_v[l].reshape(b_s, n_past, KV_DIM), sinks[l], t_p, t_s)

    c_p, w_gates_bf = _conv_prompt(
        cu, w_dw[l], b_dw[l], g_conv_ln[l], b_conv_ln[l], t_p,
        lambda n_steps: _slab_stream(w_in_l, (D_MODEL // 2, bw), 1, n_steps, col0=ga_blk0 * bw))
    cu_s_t = cu[t_p:].reshape(b_s, t_s, CONV_DIM).transpose(1, 0, 2)
    c_s_t, conv_state_t = _conv_sample(cu_s_t, state_conv[l].transpose(1, 0, 2),
                                       w_dw[l], b_dw[l], g_conv_ln[l], b_conv_ln[l])
    c_s = c_s_t.transpose(1, 0, 2).reshape(b_s * t_s, CONV_DIM)

    btm = BIG_TOKEN_TILE
    gate_tn = 2 * bw
    n_gate_steps = 2 * D_MODEL // gate_tn
    gate_grid = (n_gate_steps, m // btm)
    gates, w_attn_o_bf, w_conv_o_bf, w_out_bf = _wres_matmul(
        [(h, w_gates_bf, gate_tn, [lambda j: j], None)],
        [],
        [(sds((m, 2 * D_MODEL), BF16), (btm, gate_tn), lambda j, i: (i, j))],
        _epi_sigmoid, n_col_steps=n_gate_steps, name="proj_gates", tm=btm,
        side_casts=[_slab_stream(w_attn_o[l], (HEAD_DIM, D_MODEL), *gate_grid, row_of_slab=_interleaved_head),
                    _slab_stream(w_conv_o[l], (CONV_DIM // 16, D_MODEL // 4), *gate_grid),
                    _slab_stream(w_out[l], (D_MODEL // 16, D_MODEL // 4), *gate_grid)])

    n_gate_blk = D_MODEL // (2 * bw)
    one_block = [lambda j: j]
    ffn_slab = (D_MODEL // 32, D_FF // 2)
    mix, w_fg_bf = _wres_matmul(
        [(Stacked(o_p, o_s), w_attn_o_bf, 2 * bw, one_block, None),
         (Stacked(c_p, c_s), w_conv_o_bf, 2 * bw, one_block, None)],
        [(gates, (tm, 2 * bw), lambda j, i: (i, j)),
         (gates, (tm, 2 * bw), lambda j, i: (i, j + n_gate_blk))],
        [(sds((m, D_MODEL), BF16), (tm, 2 * bw), lambda j, i: (i, j))],
        _epi_merge, n_col_steps=n_gate_blk, name="merge",
        side_casts=[_slab_stream(w_ffn_in[l], ffn_slab, n_gate_blk, m // tm, col0=0, n_cols=D_FF)])

    x1, w_fu_bf = _wres_matmul(
        [(mix, w_out_bf, 2 * bw, one_block, None)],
        [(x, (tm, 2 * bw), lambda j, i: (i, j))],
        [(sds((m, D_MODEL), F32), (tm, 2 * bw), lambda j, i: (i, j))],
        _epi_residual, n_col_steps=D_MODEL // (2 * bw), name="out_proj",
        side_casts=[_slab_stream(w_ffn_in[l], ffn_slab, D_MODEL // (2 * bw), m // tm, col0=D_FF, n_cols=D_FF)])

    h2 = _rmsnorm(x1, g_ffn_norm[l], BF16, "rmsnorm_ffn")
    act, w_ffn_out_bf = _swiglu_matmul(h2, w_fg_bf, w_fu_bf, w_ffn_out[l], tm=FFN_TOKEN_TILE,
                                       bw=FFN_BLOCK, name="ffn_in")

    x2 = _ares_matmul_residual(act, w_ffn_out_bf, x1, tm=tm, tn=512, name="ffn_out")
    norm_tile = 256
    y_p = _rmsnorm(x2, g_final, F32, "rmsnorm_final_prompt", row_tile0=0, n_tiles=t_p // norm_tile,
                   tm=norm_tile)
    y_s = _rmsnorm(x2, g_final, F32, "rmsnorm_final_sample", row_tile0=t_p // norm_tile,
                   n_tiles=(m - t_p) // norm_tile, tm=norm_tile)

    y_prompt = y_p.reshape(x_prompt.shape)
    y_sample = y_s.reshape(x_sample.shape)
    k_prompt = k_all[t_p - WINDOW:t_p].reshape(1, 1, WINDOW, N_KV_HEADS, HEAD_DIM)
    v_prompt = v_all[t_p - WINDOW:t_p].reshape(1, 1, WINDOW, N_KV_HEADS, HEAD_DIM)
    conv_prompt = cu[t_p - (CONV_WIDTH - 1):t_p].reshape(1, 1, CONV_WIDTH - 1, CONV_DIM)
    k_sample = k_win.reshape(1, b_s, WINDOW, N_KV_HEADS, HEAD_DIM)
    v_sample = v_win.reshape(1, b_s, WINDOW, N_KV_HEADS, HEAD_DIM)
    conv_sample = conv_state_t.transpose(1, 0, 2).reshape(1, b_s, CONV_WIDTH - 1, CONV_DIM)
    return (y_prompt, y_sample, k_prompt, v_prompt, conv_prompt, k_sample, v_sample, conv_sample)
```

```python
import collections
import functools

import jax
import jax.numpy as jnp
from jax import lax
from jax.experimental import pallas as pl
from jax.experimental.pallas import tpu as pltpu

D_MODEL = 4096
HEAD_DIM = 64
N_HEADS = 32
N_KV_HEADS = 4
N_GROUP = N_HEADS // N_KV_HEADS
WINDOW = 128
ROPE_THETA = 10000.0
CONV_DIM = 2048
CONV_WIDTH = 31
D_FF = 11008
PAST_LEN = 8192
Q_DIM = N_HEADS * HEAD_DIM
KV_DIM = N_KV_HEADS * HEAD_DIM
EPS = 1e-6
NEG = -1e30
LOG2_E = 1.4426950408889634

V7X_LANES = 128
V7X_VMEM_LIMIT_BYTES = 60000 * 1024

F32 = jnp.float32
BF16 = jnp.bfloat16

TOKEN_TILE = 512
WEIGHT_BLOCK = 512
CAST_ROWS = 256
N_KV_PAIRS = N_KV_HEADS // 2
CHUNKS_PER_PAIR = N_GROUP

Stacked = collections.namedtuple("Stacked", ["prompt", "sample"])

SideCast = collections.namedtuple("SideCast", ["array", "slab", "in_map", "out_shape", "out_map"])


def _slab_stream(array, slab, n_outer, n_inner, *, col0=0, n_cols=None, row_of_slab=None):
    k, n = array.shape
    n_cols = n - col0 if n_cols is None else n_cols
    assert k % slab[0] == 0 and n_cols % slab[1] == 0 and col0 % slab[1] == 0
    n_r, n_c = k // slab[0], n_cols // slab[1]
    assert n_r * n_c <= n_outer * n_inner, "not enough grid steps to cast every slab"

    def rc(j, i):
        t = jnp.minimum(j * n_inner + i, n_r * n_c - 1)
        return t // n_c, t % n_c

    def in_map(j, i):
        r, c = rc(j, i)
        return (r if row_of_slab is None else row_of_slab(r), col0 // slab[1] + c)

    return SideCast(array, slab, in_map, (k, n_cols), rc)


def _params(*sem):
    return pltpu.CompilerParams(dimension_semantics=sem,
                                vmem_limit_bytes=V7X_VMEM_LIMIT_BYTES)


def _stacked_specs(op, block, imap, tm):
    if not isinstance(op, Stacked):
        return [op], [pl.BlockSpec(block, imap)]
    n_p = op.prompt.shape[0] // tm
    assert op.prompt.shape[0] % tm == 0 and op.sample.shape[0] % tm == 0

    def p_map(*ids):
        r, c = imap(*ids)
        return (jnp.minimum(r, n_p - 1), c)

    def s_map(*ids):
        r, c = imap(*ids)
        return (jnp.maximum(r - n_p, 0), c)

    return [op.prompt, op.sample], [pl.BlockSpec(block, p_map), pl.BlockSpec(block, s_map)]


def _n_prompt_tiles(op, tm):
    return op.prompt.shape[0] // tm if isinstance(op, Stacked) else None


def _load(refs, n_p, row_tile):
    if n_p is None:
        return refs[0][...]
    return jnp.where(row_tile < n_p, refs[0][...], refs[1][...])


def _rmsnorm_kernel(*refs, n_p):
    x_refs, (g_ref, o_ref) = refs[:-2], refs[-2:]
    x = _load(x_refs, n_p, pl.program_id(0))
    y = x * lax.rsqrt(jnp.mean(x * x, axis=-1, keepdims=True) + EPS) * g_ref[...]
    o_ref[...] = y.astype(o_ref.dtype)


def _rmsnorm(x, g, out_dtype, name, *, row_tile0=0, n_tiles=None, tm=256):
    d = g.shape[0]
    rows = (x.prompt.shape[0] + x.sample.shape[0]) if isinstance(x, Stacked) else x.shape[0]
    n_tiles = rows // tm if n_tiles is None else n_tiles
    arrays, specs = _stacked_specs(x, (tm, d), lambda i: (i + row_tile0, 0), tm)
    return pl.pallas_call(
        functools.partial(_rmsnorm_kernel, n_p=_n_prompt_tiles(x, tm)),
        grid=(n_tiles,),
        in_specs=specs + [pl.BlockSpec((1, d), lambda i: (0, 0))],
        out_specs=pl.BlockSpec((tm, d), lambda i: (i, 0)),
        out_shape=jax.ShapeDtypeStruct((n_tiles * tm, d), out_dtype),
        compiler_params=_params("arbitrary"),
        name=name,
    )(*arrays, g.reshape(1, d))


def _cast_plain(w_refs, wbf_ref):
    k, bw = w_refs[0].shape

    def body(c, carry):
        r = pl.multiple_of(c * CAST_ROWS, CAST_ROWS)
        for b, w_ref in enumerate(w_refs):
            wbf_ref[pl.ds(r, CAST_ROWS), b * bw:(b + 1) * bw] = w_ref[pl.ds(r, CAST_ROWS), :].astype(BF16)
        return carry

    lax.fori_loop(0, k // CAST_ROWS, body, 0)


def _cast_interleave_heads(w_refs, wbf_ref):
    w_even, w_odd = w_refs
    k, bw = w_even.shape

    def body(c, carry):
        r = pl.multiple_of(c * CAST_ROWS, CAST_ROWS)
        a = w_even[pl.ds(r, CAST_ROWS), :]
        b = w_odd[pl.ds(r, CAST_ROWS), :]
        pieces = []
        for g in range(bw // HEAD_DIM):
            pieces += [a[:, g * HEAD_DIM:(g + 1) * HEAD_DIM], b[:, g * HEAD_DIM:(g + 1) * HEAD_DIM]]
        wbf_ref[pl.ds(r, CAST_ROWS), :] = jnp.concatenate(pieces, axis=1).astype(BF16)
        return carry

    lax.fori_loop(0, k // CAST_ROWS, body, 0)


def _interleaved_head(slot):
    chunk, parity = divmod(slot, 2)
    pair, g = divmod(chunk, CHUNKS_PER_PAIR)
    return (2 * pair + parity) * N_GROUP + g


def _wres_kernel(*refs, layout, casts, epilogue):
    a_counts, w_counts, extra_counts, n_out, n_side, a_np, extra_np = layout
    pos = 0

    def take(counts):
        nonlocal pos
        groups = []
        for n in counts:
            groups.append(refs[pos:pos + n])
            pos += n
        return groups

    a_groups = take(a_counts)
    w_groups = take(w_counts)
    extra_groups = take(extra_counts)
    side_in = refs[pos:pos + n_side]
    pos += n_side
    out_refs = refs[pos:pos + n_out]
    pos += n_out
    side_out = refs[pos:pos + n_side]
    scratch_refs = list(refs[pos + n_side:])
    wbf_refs = [scratch_refs.pop(0) if cast is not None else None for cast in casts]
    i = pl.program_id(1)

    if any(cast is not None for cast in casts):
        @pl.when(i == 0)
        def _():
            for cast, w_refs, wbf_ref in zip(casts, w_groups, wbf_refs):
                if cast is not None:
                    cast(w_refs, wbf_ref)

    for s_in, s_out in zip(side_in, side_out):
        s_out[...] = s_in[...].astype(s_out.dtype)

    accs = []
    for a_refs, n_p, w_refs, wbf_ref in zip(a_groups, a_np, w_groups, wbf_refs):
        a = _load(a_refs, n_p, i)
        if wbf_ref is not None:
            accs.append(jnp.dot(a, wbf_ref[...], preferred_element_type=F32))
        else:
            parts = [jnp.dot(a, w_ref[...], preferred_element_type=F32) for w_ref in w_refs]
            accs.append(parts[0] if len(parts) == 1 else jnp.concatenate(parts, axis=1))
    extras = [_load(e_refs, n_p, i) for e_refs, n_p in zip(extra_groups, extra_np)]
    for o_ref, o in zip(out_refs, epilogue(accs, extras)):
        o_ref[...] = o.astype(o_ref.dtype)


def _wres_matmul(terms, extras, outs, epilogue, *, n_col_steps, name, tm=TOKEN_TILE,
                 weight_buffers=2, side_casts=()):
    a0 = terms[0][0]
    m = (a0.prompt.shape[0] + a0.sample.shape[0]) if isinstance(a0, Stacked) else a0.shape[0]
    assert m % tm == 0
    in_arrays, in_specs, scratch, casts = [], [], [], []
    a_counts, w_counts, extra_counts, a_np, extra_np = [], [], [], [], []
    for a, w, bw, col_fns, cast in terms:
        k = (w[0] if isinstance(w, tuple) else w).shape[0]
        arrs, specs = _stacked_specs(a, (tm, k), lambda j, i: (i, 0), tm)
        in_arrays += arrs
        in_specs += specs
        a_counts.append(len(arrs))
        a_np.append(_n_prompt_tiles(a, tm))
    for a, w, bw, col_fns, cast in terms:
        w_of_block = w if isinstance(w, tuple) else (w,) * len(col_fns)
        k = w_of_block[0].shape[0]
        for fn, w_blk in zip(col_fns, w_of_block):
            in_arrays.append(w_blk)
            in_specs.append(pl.BlockSpec((k, bw), functools.partial(lambda j, i, fn: (0, fn(j)), fn=fn),
                                         pipeline_mode=pl.Buffered(weight_buffers)))
        w_counts.append(len(col_fns))
        casts.append(cast)
        if cast is not None:
            scratch.append(pltpu.VMEM((k, bw * len(col_fns)), BF16))
    for op, blk, imap in extras:
        arrs, specs = _stacked_specs(op, blk, imap, tm)
        in_arrays += arrs
        in_specs += specs
        extra_counts.append(len(arrs))
        extra_np.append(_n_prompt_tiles(op, tm))
    for sc in side_casts:
        in_arrays.append(sc.array)
        in_specs.append(pl.BlockSpec(sc.slab, sc.in_map))
    layout = (tuple(a_counts), tuple(w_counts), tuple(extra_counts), len(outs), len(side_casts),
              tuple(a_np), tuple(extra_np))
    return pl.pallas_call(
        functools.partial(_wres_kernel, layout=layout, casts=tuple(casts), epilogue=epilogue),
        grid=(n_col_steps, m // tm),
        in_specs=in_specs,
        out_specs=([pl.BlockSpec(blk, imap) for _, blk, imap in outs]
                   + [pl.BlockSpec(sc.slab, sc.out_map) for sc in side_casts]),
        out_shape=([s for s, _, _ in outs]
                   + [jax.ShapeDtypeStruct(sc.out_shape, BF16) for sc in side_casts]),
        scratch_shapes=scratch,
        compiler_params=_params("arbitrary", "arbitrary"),
        name=name,
    )(*in_arrays)


def _ares_kernel(a_ref, w_ref, x_ref, o_ref):
    acc = jnp.dot(a_ref[...], w_ref[...], preferred_element_type=F32)
    o_ref[...] = x_ref[...] + acc


def _ares_matmul_residual(a, w_bf, x, *, tm, tn, name):
    m, k = a.shape
    n = w_bf.shape[1]
    return pl.pallas_call(
        _ares_kernel,
        grid=(m // tm, n // tn),
        in_specs=[pl.BlockSpec((tm, k), lambda i, j: (i, 0)),
                  pl.BlockSpec((k, tn), lambda i, j: (0, j)),
                  pl.BlockSpec((tm, tn), lambda i, j: (i, j))],
        out_specs=pl.BlockSpec((tm, tn), lambda i, j: (i, j)),
        out_shape=jax.ShapeDtypeStruct((m, n), F32),
        compiler_params=_params("arbitrary", "arbitrary"),
        name=name,
    )(a, w_bf, x)


def _rope(x, cos128, sin128):
    n = x.shape[1]
    reps = n // V7X_LANES
    c = jnp.concatenate([cos128] * reps, axis=1) if reps > 1 else cos128
    s = jnp.concatenate([sin128] * reps, axis=1) if reps > 1 else sin128
    lane = lax.broadcasted_iota(jnp.int32, x.shape, 1)
    first_half = (lane & (HEAD_DIM - 1)) < (HEAD_DIM // 2)
    partner = jnp.where(first_half,
                        pltpu.roll(x, n - HEAD_DIM // 2, axis=1),
                        pltpu.roll(x, HEAD_DIM // 2, axis=1))
    return x * c + partner * s


def _sigmoid(x):
    return 0.5 * jnp.tanh(0.5 * x) + 0.5


def _epi_q(accs, extras):
    cos, sin = extras
    return [_rope(accs[0], cos, sin)]


def _epi_kv(accs, extras):
    cos, sin = extras
    acc = accs[0]
    return [_rope(acc[:, :KV_DIM], cos, sin), acc[:, KV_DIM:]]


def _epi_glu(accs, extras):
    acc = accs[0]
    half = acc.shape[1] // 2
    return [acc[:, :half] * _sigmoid(acc[:, half:])]


def _epi_sigmoid(accs, extras):
    return [_sigmoid(accs[0])]


def _epi_swiglu(accs, extras):
    acc = accs[0]
    half = acc.shape[1] // 2
    return [acc[:, :half] * _sigmoid(acc[:, :half]) * acc[:, half:]]


def _epi_merge(accs, extras):
    g_a, g_c = extras
    return [g_a.astype(F32) * accs[0] + g_c.astype(F32) * accs[1]]


def _epi_residual(accs, extras):
    return [extras[0] + accs[0]]


def _attend(qs, kx, vx_ones, valid, sk, n_rep):
    r, nk = qs.shape[0], kx.shape[0]
    s = lax.dot_general(qs, kx, (((1,), (1,)), ((), ())), preferred_element_type=F32)
    s = jnp.where(valid[None], s.reshape(n_rep, r // n_rep, nk), NEG).reshape(r, nk)
    m = jnp.maximum(jnp.broadcast_to(jnp.max(s, axis=-1, keepdims=True), sk.shape), sk)
    p = jnp.exp2(s - jnp.concatenate([m] * (nk // V7X_LANES), axis=1)).astype(BF16)
    ol = jnp.dot(p, vx_ones, preferred_element_type=F32)
    return ol[:, :V7X_LANES] / (ol[:, V7X_LANES:] + jnp.exp2(sk - m))


def _attend_all_heads(q_ref, o_ref, sink_ref, k_of_pair, v_of_pair, valid):
    rows = q_ref.shape[0]
    nk = valid.shape[1]
    lane = lax.broadcasted_iota(jnp.int32, (nk, V7X_LANES), 1)
    ones = jnp.ones((nk, V7X_LANES), BF16)
    scale = HEAD_DIM ** -0.5 * LOG2_E
    for pair in range(N_KV_PAIRS):
        kcol = k_of_pair(pair) * scale
        vcol = v_of_pair(pair)
        c0 = pair * CHUNKS_PER_PAIR
        qs = jnp.concatenate([q_ref[:, (c0 + g) * V7X_LANES:(c0 + g + 1) * V7X_LANES]
                              for g in range(CHUNKS_PER_PAIR)], axis=0)
        o_pair = None
        for parity in range(2):
            half = (lane < HEAD_DIM) if parity == 0 else (lane >= HEAD_DIM)
            kx = jnp.where(half, kcol, 0.0).astype(BF16)
            vx = jnp.concatenate([jnp.where(half, vcol, 0.0).astype(BF16), ones], axis=1)
            o = _attend(qs, kx, vx, valid, sink_ref[2 * pair + parity], CHUNKS_PER_PAIR)
            o_pair = o if o_pair is None else o_pair + o
        for g in range(CHUNKS_PER_PAIR):
            o_ref[:, (c0 + g) * V7X_LANES:(c0 + g + 1) * V7X_LANES] = (
                o_pair[g * rows:(g + 1) * rows].astype(o_ref.dtype))


def _sink_table(sinks, rows):
    per_row = jnp.repeat(sinks.astype(F32).reshape(N_KV_HEADS, N_GROUP) * LOG2_E, rows, axis=1)
    return jnp.broadcast_to(per_row[:, :, None], (N_KV_HEADS, N_GROUP * rows, V7X_LANES))


def _attn_prompt_kernel(sink_ref, q_ref, kc_ref, kp_ref, vc_ref, vp_ref, o_ref):
    n = pl.program_id(0)
    blk = q_ref.shape[0]
    row = lax.broadcasted_iota(jnp.int32, (blk, 2 * blk), 0)
    col = lax.broadcasted_iota(jnp.int32, (blk, 2 * blk), 1)
    diff = row + blk - col
    valid = (diff >= 0) & (diff < WINDOW) & ((n > 0) | (col >= blk))

    def both_blocks(prev_ref, cur_ref):
        return lambda pair: jnp.concatenate(
            [prev_ref[:, pair * V7X_LANES:(pair + 1) * V7X_LANES],
             cur_ref[:, pair * V7X_LANES:(pair + 1) * V7X_LANES]], axis=0)

    _attend_all_heads(q_ref, o_ref, sink_ref, both_blocks(kp_ref, kc_ref), both_blocks(vp_ref, vc_ref), valid)


def _attn_prompt(q, k, v, sinks, t_prompt):
    blk = WINDOW
    nb = t_prompt // blk
    cur = lambda n: (n, 0)
    prev = lambda n: (jnp.maximum(n - 1, 0), 0)
    return pl.pallas_call(
        _attn_prompt_kernel,
        grid=(nb,),
        in_specs=[pl.BlockSpec((N_KV_HEADS, N_GROUP * blk, V7X_LANES), lambda n: (0, 0, 0)),
                  pl.BlockSpec((blk, Q_DIM), cur),
                  pl.BlockSpec((blk, KV_DIM), cur),
                  pl.BlockSpec((blk, KV_DIM), prev),
                  pl.BlockSpec((blk, KV_DIM), cur),
                  pl.BlockSpec((blk, KV_DIM), prev)],
        out_specs=pl.BlockSpec((blk, Q_DIM), cur),
        out_shape=jax.ShapeDtypeStruct((t_prompt, Q_DIM), BF16),
        compiler_params=_params("arbitrary"),
        name="attn_prompt",
    )(_sink_table(sinks, blk), q, k, k, v, v)


SAMPLE_BATCH_BLOCK = 8
NEW_KEY_PAD = 128


def _attn_sample_kernel(sink_ref, q_ref, kn_ref, vn_ref, ck_ref, cv_ref, o_ref, ko_ref, vo_ref):
    bb, n_past = ck_ref.shape[0], ck_ref.shape[1]
    rows = q_ref.shape[0]
    t_new = rows // bb
    n_cache = bb * n_past
    nk = n_cache + NEW_KEY_PAD
    lg_t, lg_p = t_new.bit_length() - 1, n_past.bit_length() - 1

    for b in range(bb):
        ko_ref[b, 0:n_past - t_new, :] = ck_ref[b, t_new:n_past, :]
        ko_ref[b, n_past - t_new:n_past, :] = kn_ref[b * t_new:(b + 1) * t_new, :]
        vo_ref[b, 0:n_past - t_new, :] = cv_ref[b, t_new:n_past, :]
        vo_ref[b, n_past - t_new:n_past, :] = vn_ref[b * t_new:(b + 1) * t_new, :]

    r = lax.broadcasted_iota(jnp.int32, (rows, nk), 0)
    c = lax.broadcasted_iota(jnp.int32, (rows, nk), 1)
    b_r, t = r >> lg_t, r & (t_new - 1)
    d_cache = t + n_past - (c & (n_past - 1))
    valid_cache = (c < n_cache) & ((c >> lg_p) == b_r) & (d_cache >= 0) & (d_cache < WINDOW)
    cn = c - n_cache
    d_new = t - (cn & (t_new - 1))
    valid_new = (cn >= 0) & (cn < rows) & ((cn >> lg_t) == b_r) & (d_new >= 0) & (d_new < WINDOW)
    valid = valid_cache | valid_new

    pad = jnp.zeros((NEW_KEY_PAD - rows, V7X_LANES), F32)

    def keys(cache_ref, new_ref):
        def of_pair(pair):
            ls = slice(pair * V7X_LANES, (pair + 1) * V7X_LANES)
            return jnp.concatenate([cache_ref[:, :, ls].reshape(n_cache, V7X_LANES), new_ref[:, ls], pad], axis=0)
        return of_pair

    _attend_all_heads(q_ref, o_ref, sink_ref, keys(ck_ref, kn_ref), keys(cv_ref, vn_ref), valid)


def _attn_sample(q, k, v, cache_k, cache_v, sinks, row0, t_new):
    b, n_past, _ = cache_k.shape
    assert n_past == WINDOW and t_new & (t_new - 1) == 0
    bb = SAMPLE_BATCH_BLOCK
    rows = bb * t_new
    assert row0 % rows == 0 and rows <= NEW_KEY_PAD
    tok = lambda i: (row0 // rows + i, 0)
    i3 = lambda i: (i, 0, 0)
    return pl.pallas_call(
        _attn_sample_kernel,
        grid=(b // bb,),
        in_specs=[pl.BlockSpec((N_KV_HEADS, N_GROUP * rows, V7X_LANES), lambda i: (0, 0, 0)),
                  pl.BlockSpec((rows, Q_DIM), tok),
                  pl.BlockSpec((rows, KV_DIM), tok),
                  pl.BlockSpec((rows, KV_DIM), tok),
                  pl.BlockSpec((bb, n_past, KV_DIM), i3),
                  pl.BlockSpec((bb, n_past, KV_DIM), i3)],
        out_specs=[pl.BlockSpec((rows, Q_DIM), lambda i: (i, 0)),
                   pl.BlockSpec((bb, n_past, KV_DIM), i3),
                   pl.BlockSpec((bb, n_past, KV_DIM), i3)],
        out_shape=[jax.ShapeDtypeStruct((b * t_new, Q_DIM), BF16),
                   jax.ShapeDtypeStruct((b, n_past, KV_DIM), F32),
                   jax.ShapeDtypeStruct((b, n_past, KV_DIM), F32)],
        compiler_params=_params("arbitrary"),
        name="attn_sample",
    )(_sink_table(sinks, rows), q, k, v, cache_k, cache_v)


def _ln_swish(c, g, b):
    mu = jnp.mean(c, axis=-1, keepdims=True)
    var = jnp.mean(jnp.square(c - mu), axis=-1, keepdims=True)
    y = (c - mu) * lax.rsqrt(var + EPS) * g + b
    return jax.nn.silu(y)


CONV_HALO = 32
SUBLANES = 8
CONV_ROW_TILE = 128


def _conv_prompt_kernel(cur_ref, halo_ref, w_ref, b_ref, g_ref, bl_ref, side_ref, o_ref, side_o_ref,
                        xs_ref, c_ref):
    i = pl.program_id(0)
    tt = cur_ref.shape[0]
    side_o_ref[...] = side_ref[...].astype(side_o_ref.dtype)
    xs_ref[0:CONV_HALO, :] = jnp.where(i > 0, halo_ref[...], 0.0)
    xs_ref[CONV_HALO:CONV_HALO + tt, :] = cur_ref[...]
    off = CONV_HALO - (CONV_WIDTH - 1)
    n_groups = CONV_ROW_TILE // SUBLANES
    taps_by_shift = [[(w, (off + w) // SUBLANES) for w in range(CONV_WIDTH) if (off + w) % SUBLANES == s]
                     for s in range(SUBLANES)]
    n_in = max(n_groups + max(d for _, d in taps) + (s > 0) for s, taps in enumerate(taps_by_shift))
    assert tt - CONV_ROW_TILE + n_in * SUBLANES <= CONV_HALO + tt
    sub = lax.broadcasted_iota(jnp.int32, (SUBLANES, V7X_LANES), 0)

    def lane_chunk(cc, carry):
        ls = pl.ds(pl.multiple_of(cc * V7X_LANES, V7X_LANES), V7X_LANES)
        for r0 in range(0, tt, CONV_ROW_TILE):
            x0 = [xs_ref[pl.ds(r0 + kg * SUBLANES, SUBLANES), ls] for kg in range(n_in)]
            acc = [None] * n_groups
            for s, taps in enumerate(taps_by_shift):
                wv = {w: jnp.broadcast_to(w_ref[w:w + 1, ls], (SUBLANES, V7X_LANES)) for w, _ in taps}
                n_used = n_groups + max(d for _, d in taps)
                if s == 0:
                    xsft = x0
                else:
                    rot = [pltpu.roll(x0[kg], SUBLANES - s, axis=0) for kg in range(n_used + 1)]
                    xsft = [jnp.where(sub < SUBLANES - s, rot[kg], rot[kg + 1]) for kg in range(n_used)]
                for kg in range(n_used):
                    for w, d in taps:
                        og = kg - d
                        if 0 <= og < n_groups:
                            term = xsft[kg] * wv[w]
                            acc[og] = term if acc[og] is None else acc[og] + term
            bias = b_ref[:, ls]
            for og in range(n_groups):
                c_ref[pl.ds(r0 + og * SUBLANES, SUBLANES), ls] = acc[og] + bias
        return carry

    lax.fori_loop(0, cur_ref.shape[1] // V7X_LANES, lane_chunk, 0)
    o_ref[...] = _ln_swish(c_ref[...], g_ref[...], bl_ref[...]).astype(o_ref.dtype)


def _conv_prompt(cu, w_dw, b_dw, g_ln, b_ln, t_prompt, make_side_cast):
    tt = 256
    c = cu.shape[1]
    per = tt // CONV_HALO
    row1 = lambda i: (0, 0)
    side = make_side_cast(t_prompt // tt)
    return pl.pallas_call(
        _conv_prompt_kernel,
        grid=(t_prompt // tt,),
        in_specs=[pl.BlockSpec((tt, c), lambda i: (i, 0)),
                  pl.BlockSpec((CONV_HALO, c), lambda i: (jnp.maximum(i * per - 1, 0), 0)),
                  pl.BlockSpec((CONV_WIDTH, c), row1),
                  pl.BlockSpec((1, c), row1),
                  pl.BlockSpec((1, c), row1),
                  pl.BlockSpec((1, c), row1),
                  pl.BlockSpec(side.slab, lambda i: side.in_map(0, i))],
        out_specs=[pl.BlockSpec((tt, c), lambda i: (i, 0)),
                   pl.BlockSpec(side.slab, lambda i: side.out_map(0, i))],
        out_shape=[jax.ShapeDtypeStruct((t_prompt, c), BF16),
                   jax.ShapeDtypeStruct(side.out_shape, BF16)],
        scratch_shapes=[pltpu.VMEM((CONV_HALO + tt, c), F32), pltpu.VMEM((tt, c), F32)],
        compiler_params=_params("arbitrary"),
        name="conv_prompt",
    )(cu, cu, w_dw, b_dw.reshape(1, c), g_ln.reshape(1, c), b_ln.reshape(1, c), side.array)


CONV_SAMPLE_BATCH_TILE = 32


def _conv_sample_kernel(st_ref, cu_ref, w_ref, b_ref, g_ref, bl_ref, o_ref, so_ref, c_ref):
    n_hist, bt, c = st_ref.shape
    t_new = cu_ref.shape[0]

    def lane_chunk(cc, carry):
        ls = pl.ds(pl.multiple_of(cc * V7X_LANES, V7X_LANES), V7X_LANES)
        bias = b_ref[:, ls]
        for r0 in range(0, bt, SUBLANES):
            rs = slice(r0, r0 + SUBLANES)
            xs = [st_ref[r, rs, ls] for r in range(n_hist)] + [cu_ref[t, rs, ls] for t in range(t_new)]
            for t in range(t_new):
                acc = xs[t] * w_ref[0:1, ls]
                for w in range(1, CONV_WIDTH):
                    acc = acc + xs[t + w] * w_ref[w:w + 1, ls]
                c_ref[t, rs, ls] = acc + bias
            for r in range(n_hist):
                so_ref[r, rs, ls] = xs[r + t_new]
        return carry

    lax.fori_loop(0, c // V7X_LANES, lane_chunk, 0)
    for t in range(t_new):
        o_ref[t] = _ln_swish(c_ref[t], g_ref[...], bl_ref[...]).astype(o_ref.dtype)


def _conv_sample(cu_t, state_t, w_dw, b_dw, g_ln, b_ln):
    n_hist, b, c = state_t.shape
    t_new = cu_t.shape[0]
    assert n_hist == CONV_WIDTH - 1
    bt = CONV_SAMPLE_BATCH_TILE
    blk = lambda i: (0, i, 0)
    row1 = lambda i: (0, 0)
    return pl.pallas_call(
        _conv_sample_kernel,
        grid=(b // bt,),
        in_specs=[pl.BlockSpec((n_hist, bt, c), blk),
                  pl.BlockSpec((t_new, bt, c), blk),
                  pl.BlockSpec((CONV_WIDTH, c), row1),
                  pl.BlockSpec((1, c), row1),
                  pl.BlockSpec((1, c), row1),
                  pl.BlockSpec((1, c), row1)],
        out_specs=[pl.BlockSpec((t_new, bt, c), blk),
                   pl.BlockSpec((n_hist, bt, c), blk)],
        out_shape=[jax.ShapeDtypeStruct((t_new, b, c), BF16),
                   jax.ShapeDtypeStruct((n_hist, b, c), F32)],
        scratch_shapes=[pltpu.VMEM((t_new, bt, c), F32)],
        compiler_params=_params("arbitrary"),
        name="conv_sample",
    )(state_t, cu_t, w_dw, b_dw.reshape(1, c), g_ln.reshape(1, c), b_ln.reshape(1, c))


def _rope_tables(pos):
    half = HEAD_DIM // 2
    inv_freq = ROPE_THETA ** (-jnp.arange(half, dtype=F32) / half)
    ang = pos.astype(F32)[:, None] * inv_freq[None, :]
    cos, sin = jnp.cos(ang), jnp.sin(ang)
    cos64 = jnp.concatenate([cos, cos], axis=-1)
    sin64 = jnp.concatenate([-sin, sin], axis=-1)
    reps = V7X_LANES // HEAD_DIM
    return jnp.tile(cos64, (1, reps)), jnp.tile(sin64, (1, reps))


BIG_TOKEN_TILE = 1088
FFN_TOKEN_TILE = 2176
FFN_BLOCK = 256


def kernel(x_prompt, x_sample, cache_k, cache_v, state_conv, g_mix_norm, w_in, sinks, w_attn_o,
           w_dw, b_dw, g_conv_ln, b_conv_ln, w_conv_o, w_out, g_ffn_norm, w_ffn_in, w_ffn_out,
           g_final):
    depth = w_in.shape[0]
    assert depth == 1, "single-layer trunk"
    t_p = x_prompt.shape[1]
    b_s, t_s, _ = x_sample.shape
    m = t_p + b_s * t_s
    tm = TOKEN_TILE
    bw = WEIGHT_BLOCK
    sds = jax.ShapeDtypeStruct
    plain = _cast_plain

    x = Stacked(x_prompt.reshape(t_p, D_MODEL), x_sample.reshape(b_s * t_s, D_MODEL))
    pos = jnp.concatenate([jnp.arange(t_p, dtype=jnp.int32),
                           jnp.tile(PAST_LEN + jnp.arange(t_s, dtype=jnp.int32), b_s)])
    cos_t, sin_t = _rope_tables(pos)
    rope_extras = [(cos_t, (tm, V7X_LANES), lambda j, i: (i, 0)),
                   (sin_t, (tm, V7X_LANES), lambda j, i: (i, 0))]

    l = 0
    w_in_l = w_in[l]
    h = _rmsnorm(x, g_mix_norm[l], BF16, "rmsnorm_mix")

    kv_blk0 = Q_DIM // bw
    ua_blk0 = (Q_DIM + 2 * KV_DIM) // bw
    ub_blk0 = ua_blk0 + CONV_DIM // bw
    ga_blk0 = ub_blk0 + CONV_DIM // bw
    assert 2 * KV_DIM == bw and N_GROUP * HEAD_DIM == bw

    (q,) = _wres_matmul(
        [(h, w_in_l, bw, [lambda j: 2 * j, lambda j: 2 * j + 1], _cast_interleave_heads)],
        rope_extras,
        [(sds((m, Q_DIM), BF16), (tm, 2 * bw), lambda j, i: (i, j))],
        _epi_q, n_col_steps=N_KV_PAIRS, name="proj_q")

    k_all, v_all = _wres_matmul(
        [(h, w_in_l, bw, [lambda j: kv_blk0], plain)],
        rope_extras,
        [(sds((m, KV_DIM), F32), (tm, KV_DIM), lambda j, i: (i, 0)),
         (sds((m, KV_DIM), F32), (tm, KV_DIM), lambda j, i: (i, 0))],
        _epi_kv, n_col_steps=1, name="proj_kv")

    (cu,) = _wres_matmul(
        [(h, w_in_l, bw, [lambda j: ua_blk0 + j, lambda j: ub_blk0 + j], plain)],
        [],
        [(sds((m, CONV_DIM), F32), (tm, bw), lambda j, i: (i, j))],
        _epi_glu, n_col_steps=CONV_DIM // bw, name="proj_glu")

    n_past = cache_k.shape[2]
    o_p = _attn_prompt(q, k_all, v_all, sinks[l], t_p)
    o_s, k_win, v_win = _attn_sample(q, k_all, v_all,
                                     cache_k[l].reshape(b_s, n_past, KV_DIM),
                                     cache_v[l].reshape(b_s, n_past, KV_DIM), sinks[l], t_p, t_s)

    c_p, w_gates_bf = _conv_prompt(
        cu, w_dw[l], b_dw[l], g_conv_ln[l], b_conv_ln[l], t_p,
        lambda n_steps: _slab_stream(w_in_l, (D_MODEL // 2, bw), 1, n_steps, col0=ga_blk0 * bw))
    cu_s_t = cu[t_p:].reshape(b_s, t_s, CONV_DIM).transpose(1, 0, 2)
    c_s_t, conv_state_t = _conv_sample(cu_s_t, state_conv[l].transpose(1, 0, 2),
                                       w_dw[l], b_dw[l], g_conv_ln[l], b_conv_ln[l])
    c_s = c_s_t.transpose(1, 0, 2).reshape(b_s * t_s, CONV_DIM)

    btm = BIG_TOKEN_TILE
    gate_tn = 2 * bw
    n_gate_steps = 2 * D_MODEL // gate_tn
    gate_grid = (n_gate_steps, m // btm)
    gates, w_attn_o_bf, w_conv_o_bf, w_out_bf = _wres_matmul(
        [(h, w_gates_bf, gate_tn, [lambda j: j], None)],
        [],
        [(sds((m, 2 * D_MODEL), BF16), (btm, gate_tn), lambda j, i: (i, j))],
        _epi_sigmoid, n_col_steps=n_gate_steps, name="proj_gates", tm=btm,
        side_casts=[_slab_stream(w_attn_o[l], (HEAD_DIM, D_MODEL), *gate_grid, row_of_slab=_interleaved_head),
                    _slab_stream(w_conv_o[l], (CONV_DIM // 16, D_MODEL // 4), *gate_grid),
                    _slab_stream(w_out[l], (D_MODEL // 16, D_MODEL // 4), *gate_grid)])

    n_gate_blk = D_MODEL // (2 * bw)
    one_block = [lambda j: j]
    ffn_slab = (D_MODEL // 32, D_FF // 2)
    mix, w_fg_bf = _wres_matmul(
        [(Stacked(o_p, o_s), w_attn_o_bf, 2 * bw, one_block, None),
         (Stacked(c_p, c_s), w_conv_o_bf, 2 * bw, one_block, None)],
        [(gates, (tm, 2 * bw), lambda j, i: (i, j)),
         (gates, (tm, 2 * bw), lambda j, i: (i, j + n_gate_blk))],
        [(sds((m, D_MODEL), BF16), (tm, 2 * bw), lambda j, i: (i, j))],
        _epi_merge, n_col_steps=n_gate_blk, name="merge",
        side_casts=[_slab_stream(w_ffn_in[l], ffn_slab, n_gate_blk, m // tm, col0=0, n_cols=D_FF)])

    x1, w_fu_bf = _wres_matmul(
        [(mix, w_out_bf, 2 * bw, one_block, None)],
        [(x, (tm, 2 * bw), lambda j, i: (i, j))],
        [(sds((m, D_MODEL), F32), (tm, 2 * bw), lambda j, i: (i, j))],
        _epi_residual, n_col_steps=D_MODEL // (2 * bw), name="out_proj",
        side_casts=[_slab_stream(w_ffn_in[l], ffn_slab, D_MODEL // (2 * bw), m // tm, col0=D_FF, n_cols=D_FF)])

    h2 = _rmsnorm(x1, g_ffn_norm[l], BF16, "rmsnorm_ffn")
    ftm = FFN_TOKEN_TILE
    n_ff_blk = D_FF // FFN_BLOCK
    n_ff_steps = n_ff_blk * (m // ftm)
    act, w_ffn_out_bf = _wres_matmul(
        [(h2, (w_fg_bf, w_fu_bf), FFN_BLOCK, [lambda j: j, lambda j: j], None)],
        [],
        [(sds((m, D_FF), BF16), (ftm, FFN_BLOCK), lambda j, i: (i, j))],
        _epi_swiglu, n_col_steps=n_ff_blk, name="ffn_in", tm=ftm,
        side_casts=[_slab_stream(w_ffn_out[l], (D_FF // n_ff_steps, D_MODEL), n_ff_blk, m // ftm)])

    x2 = _ares_matmul_residual(act, w_ffn_out_bf, x1, tm=tm, tn=512, name="ffn_out")
    norm_tile = 256
    y_p = _rmsnorm(x2, g_final, F32, "rmsnorm_final_prompt", row_tile0=0, n_tiles=t_p // norm_tile,
                   tm=norm_tile)
    y_s = _rmsnorm(x2, g_final, F32, "rmsnorm_final_sample", row_tile0=t_p // norm_tile,
                   n_tiles=(m - t_p) // norm_tile, tm=norm_tile)

    y_prompt = y_p.reshape(x_prompt.shape)
    y_sample = y_s.reshape(x_sample.shape)
    k_prompt = k_all[t_p - WINDOW:t_p].reshape(1, 1, WINDOW, N_KV_HEADS, HEAD_DIM)
    v_prompt = v_all[t_p - WINDOW:t_p].reshape(1, 1, WINDOW, N_KV_HEADS, HEAD_DIM)
    conv_prompt = cu[t_p - (CONV_WIDTH - 1):t_p].reshape(1, 1, CONV_WIDTH - 1, CONV_DIM)
    k_sample = k_win.reshape(1, b_s, WINDOW, N_KV_HEADS, HEAD_DIM)
    v_sample = v_win.reshape(1, b_s, WINDOW, N_KV_HEADS, HEAD_DIM)
    conv_sample = conv_state_t.transpose(1, 0, 2).reshape(1, b_s, CONV_WIDTH - 1, CONV_DIM)
    return (y_prompt, y_sample, k_prompt, v_prompt, conv_prompt, k_sample, v_sample, conv_sample)
```

```python
import collections
import functools

import jax
import jax.numpy as jnp
from jax import lax
from jax.experimental import pallas as pl
from jax.experimental.pallas import tpu as pltpu

D_MODEL = 4096
HEAD_DIM = 64
N_HEADS = 32
N_KV_HEADS = 4
N_GROUP = N_HEADS // N_KV_HEADS
WINDOW = 128
ROPE_THETA = 10000.0
CONV_DIM = 2048
CONV_WIDTH = 31
D_FF = 11008
PAST_LEN = 8192
Q_DIM = N_HEADS * HEAD_DIM
KV_DIM = N_KV_HEADS * HEAD_DIM
EPS = 1e-6
NEG = -1e30
LOG2_E = 1.4426950408889634

V7X_LANES = 128
V7X_VMEM_LIMIT_BYTES = 60000 * 1024

F32 = jnp.float32
BF16 = jnp.bfloat16

TOKEN_TILE = 512
WEIGHT_BLOCK = 512
CAST_ROWS = 256
N_KV_PAIRS = N_KV_HEADS // 2
CHUNKS_PER_PAIR = N_GROUP

Stacked = collections.namedtuple("Stacked", ["prompt", "sample"])

SideCast = collections.namedtuple("SideCast", ["array", "slab", "in_map", "out_shape", "out_map"])


def _slab_stream(array, slab, n_outer, n_inner, *, col0=0, n_cols=None, row_of_slab=None):
    k, n = array.shape
    n_cols = n - col0 if n_cols is None else n_cols
    assert k % slab[0] == 0 and n_cols % slab[1] == 0 and col0 % slab[1] == 0
    n_r, n_c = k // slab[0], n_cols // slab[1]
    assert n_r * n_c <= n_outer * n_inner, "not enough grid steps to cast every slab"

    def rc(j, i):
        t = jnp.minimum(j * n_inner + i, n_r * n_c - 1)
        return t // n_c, t % n_c

    def in_map(j, i):
        r, c = rc(j, i)
        return (r if row_of_slab is None else row_of_slab(r), col0 // slab[1] + c)

    return SideCast(array, slab, in_map, (k, n_cols), rc)


def _params(*sem):
    return pltpu.CompilerParams(dimension_semantics=sem,
                                vmem_limit_bytes=V7X_VMEM_LIMIT_BYTES)


def _stacked_specs(op, block, imap, tm):
    if not isinstance(op, Stacked):
        return [op], [pl.BlockSpec(block, imap)]
    n_p = op.prompt.shape[0] // tm
    assert op.prompt.shape[0] % tm == 0 and op.sample.shape[0] % tm == 0

    def p_map(*ids):
        r, c = imap(*ids)
        return (jnp.minimum(r, n_p - 1), c)

    def s_map(*ids):
        r, c = imap(*ids)
        return (jnp.maximum(r - n_p, 0), c)

    return [op.prompt, op.sample], [pl.BlockSpec(block, p_map), pl.BlockSpec(block, s_map)]


def _n_prompt_tiles(op, tm):
    return op.prompt.shape[0] // tm if isinstance(op, Stacked) else None


def _load(refs, n_p, row_tile):
    if n_p is None:
        return refs[0][...]
    return jnp.where(row_tile < n_p, refs[0][...], refs[1][...])


def _rmsnorm_kernel(*refs, n_p):
    x_refs, (g_ref, o_ref) = refs[:-2], refs[-2:]
    x = _load(x_refs, n_p, pl.program_id(0))
    y = x * lax.rsqrt(jnp.mean(x * x, axis=-1, keepdims=True) + EPS) * g_ref[...]
    o_ref[...] = y.astype(o_ref.dtype)


NORM_TILE = 512


def _rmsnorm(x, g, out_dtype, name, *, row_tile0=0, n_tiles=None, tm=NORM_TILE):
    d = g.shape[0]
    rows = (x.prompt.shape[0] + x.sample.shape[0]) if isinstance(x, Stacked) else x.shape[0]
    n_tiles = rows // tm if n_tiles is None else n_tiles
    arrays, specs = _stacked_specs(x, (tm, d), lambda i: (i + row_tile0, 0), tm)
    return pl.pallas_call(
        functools.partial(_rmsnorm_kernel, n_p=_n_prompt_tiles(x, tm)),
        grid=(n_tiles,),
        in_specs=specs + [pl.BlockSpec((1, d), lambda i: (0, 0))],
        out_specs=pl.BlockSpec((tm, d), lambda i: (i, 0)),
        out_shape=jax.ShapeDtypeStruct((n_tiles * tm, d), out_dtype),
        compiler_params=_params("arbitrary"),
        name=name,
    )(*arrays, g.reshape(1, d))


def _cast_plain(w_refs, wbf_ref):
    k, bw = w_refs[0].shape

    def body(c, carry):
        r = pl.multiple_of(c * CAST_ROWS, CAST_ROWS)
        for b, w_ref in enumerate(w_refs):
            wbf_ref[pl.ds(r, CAST_ROWS), b * bw:(b + 1) * bw] = w_ref[pl.ds(r, CAST_ROWS), :].astype(BF16)
        return carry

    lax.fori_loop(0, k // CAST_ROWS, body, 0)


def _cast_interleave_heads(w_refs, wbf_ref):
    w_even, w_odd = w_refs
    k, bw = w_even.shape

    def body(c, carry):
        r = pl.multiple_of(c * CAST_ROWS, CAST_ROWS)
        a = w_even[pl.ds(r, CAST_ROWS), :]
        b = w_odd[pl.ds(r, CAST_ROWS), :]
        pieces = []
        for g in range(bw // HEAD_DIM):
            pieces += [a[:, g * HEAD_DIM:(g + 1) * HEAD_DIM], b[:, g * HEAD_DIM:(g + 1) * HEAD_DIM]]
        wbf_ref[pl.ds(r, CAST_ROWS), :] = jnp.concatenate(pieces, axis=1).astype(BF16)
        return carry

    lax.fori_loop(0, k // CAST_ROWS, body, 0)


def _interleaved_head(slot):
    chunk, parity = divmod(slot, 2)
    pair, g = divmod(chunk, CHUNKS_PER_PAIR)
    return (2 * pair + parity) * N_GROUP + g


def _wres_kernel(*refs, layout, casts, epilogue):
    a_counts, w_counts, extra_counts, n_out, n_side, a_np, extra_np = layout
    pos = 0

    def take(counts):
        nonlocal pos
        groups = []
        for n in counts:
            groups.append(refs[pos:pos + n])
            pos += n
        return groups

    a_groups = take(a_counts)
    w_groups = take(w_counts)
    extra_groups = take(extra_counts)
    side_in = refs[pos:pos + n_side]
    pos += n_side
    out_refs = refs[pos:pos + n_out]
    pos += n_out
    side_out = refs[pos:pos + n_side]
    scratch_refs = list(refs[pos + n_side:])
    wbf_refs = [scratch_refs.pop(0) if cast is not None else None for cast in casts]
    i = pl.program_id(1)

    if any(cast is not None for cast in casts):
        @pl.when(i == 0)
        def _():
            for cast, w_refs, wbf_ref in zip(casts, w_groups, wbf_refs):
                if cast is not None:
                    cast(w_refs, wbf_ref)

    for s_in, s_out in zip(side_in, side_out):
        s_out[...] = s_in[...].astype(s_out.dtype)

    accs = []
    for a_refs, n_p, w_refs, wbf_ref in zip(a_groups, a_np, w_groups, wbf_refs):
        a = _load(a_refs, n_p, i)
        if wbf_ref is not None:
            accs.append(jnp.dot(a, wbf_ref[...], preferred_element_type=F32))
        else:
            parts = [jnp.dot(a, w_ref[...], preferred_element_type=F32) for w_ref in w_refs]
            accs.append(parts[0] if len(parts) == 1 else jnp.concatenate(parts, axis=1))
    extras = [_load(e_refs, n_p, i) for e_refs, n_p in zip(extra_groups, extra_np)]
    for o_ref, o in zip(out_refs, epilogue(accs, extras)):
        o_ref[...] = o.astype(o_ref.dtype)


def _wres_matmul(terms, extras, outs, epilogue, *, n_col_steps, name, tm=TOKEN_TILE,
                 weight_buffers=2, side_casts=()):
    a0 = terms[0][0]
    m = (a0.prompt.shape[0] + a0.sample.shape[0]) if isinstance(a0, Stacked) else a0.shape[0]
    assert m % tm == 0
    in_arrays, in_specs, scratch, casts = [], [], [], []
    a_counts, w_counts, extra_counts, a_np, extra_np = [], [], [], [], []
    for a, w, bw, col_fns, cast in terms:
        k = (w[0] if isinstance(w, tuple) else w).shape[0]
        arrs, specs = _stacked_specs(a, (tm, k), lambda j, i: (i, 0), tm)
        in_arrays += arrs
        in_specs += specs
        a_counts.append(len(arrs))
        a_np.append(_n_prompt_tiles(a, tm))
    for a, w, bw, col_fns, cast in terms:
        w_of_block = w if isinstance(w, tuple) else (w,) * len(col_fns)
        k = w_of_block[0].shape[0]
        for fn, w_blk in zip(col_fns, w_of_block):
            in_arrays.append(w_blk)
            in_specs.append(pl.BlockSpec((k, bw), functools.partial(lambda j, i, fn: (0, fn(j)), fn=fn),
                                         pipeline_mode=pl.Buffered(weight_buffers)))
        w_counts.append(len(col_fns))
        casts.append(cast)
        if cast is not None:
            scratch.append(pltpu.VMEM((k, bw * len(col_fns)), BF16))
    for op, blk, imap in extras:
        arrs, specs = _stacked_specs(op, blk, imap, tm)
        in_arrays += arrs
        in_specs += specs
        extra_counts.append(len(arrs))
        extra_np.append(_n_prompt_tiles(op, tm))
    for sc in side_casts:
        in_arrays.append(sc.array)
        in_specs.append(pl.BlockSpec(sc.slab, sc.in_map))
    layout = (tuple(a_counts), tuple(w_counts), tuple(extra_counts), len(outs), len(side_casts),
              tuple(a_np), tuple(extra_np))
    return pl.pallas_call(
        functools.partial(_wres_kernel, layout=layout, casts=tuple(casts), epilogue=epilogue),
        grid=(n_col_steps, m // tm),
        in_specs=in_specs,
        out_specs=([pl.BlockSpec(blk, imap) for _, blk, imap in outs]
                   + [pl.BlockSpec(sc.slab, sc.out_map) for sc in side_casts]),
        out_shape=([s for s, _, _ in outs]
                   + [jax.ShapeDtypeStruct(sc.out_shape, BF16) for sc in side_casts]),
        scratch_shapes=scratch,
        compiler_params=_params("arbitrary", "arbitrary"),
        name=name,
    )(*in_arrays)


def _ares_kernel(a_ref, w_ref, x_ref, o_ref):
    acc = jnp.dot(a_ref[...], w_ref[...], preferred_element_type=F32)
    o_ref[...] = x_ref[...] + acc


def _ares_matmul_residual(a, w_bf, x, *, tm, tn, name):
    m, k = a.shape
    n = w_bf.shape[1]
    return pl.pallas_call(
        _ares_kernel,
        grid=(m // tm, n // tn),
        in_specs=[pl.BlockSpec((tm, k), lambda i, j: (i, 0)),
                  pl.BlockSpec((k, tn), lambda i, j: (0, j)),
                  pl.BlockSpec((tm, tn), lambda i, j: (i, j))],
        out_specs=pl.BlockSpec((tm, tn), lambda i, j: (i, j)),
        out_shape=jax.ShapeDtypeStruct((m, n), F32),
        compiler_params=_params("arbitrary", "arbitrary"),
        name=name,
    )(a, w_bf, x)


def _rope(x, cos128, sin128):
    n = x.shape[1]
    reps = n // V7X_LANES
    c = jnp.concatenate([cos128] * reps, axis=1) if reps > 1 else cos128
    s = jnp.concatenate([sin128] * reps, axis=1) if reps > 1 else sin128
    lane = lax.broadcasted_iota(jnp.int32, x.shape, 1)
    first_half = (lane & (HEAD_DIM - 1)) < (HEAD_DIM // 2)
    partner = jnp.where(first_half,
                        pltpu.roll(x, n - HEAD_DIM // 2, axis=1),
                        pltpu.roll(x, HEAD_DIM // 2, axis=1))
    return x * c + partner * s


def _sigmoid(x):
    return 0.5 * jnp.tanh(0.5 * x) + 0.5


def _epi_q(accs, extras):
    cos, sin = extras
    return [_rope(accs[0], cos, sin)]


def _epi_kv(accs, extras):
    cos, sin = extras
    acc = accs[0]
    return [_rope(acc[:, :KV_DIM], cos, sin), acc[:, KV_DIM:]]


def _epi_glu(accs, extras):
    acc = accs[0]
    half = acc.shape[1] // 2
    return [acc[:, :half] * _sigmoid(acc[:, half:])]


def _epi_sigmoid(accs, extras):
    return [_sigmoid(accs[0])]


def _epi_swiglu(accs, extras):
    acc = accs[0]
    half = acc.shape[1] // 2
    return [acc[:, :half] * _sigmoid(acc[:, :half]) * acc[:, half:]]


def _epi_merge(accs, extras):
    g_a, g_c = extras
    return [g_a.astype(F32) * accs[0] + g_c.astype(F32) * accs[1]]


def _epi_residual(accs, extras):
    return [extras[0] + accs[0]]


def _attend(qs, kx, vx_ones, valid, sk, n_rep):
    r, nk = qs.shape[0], kx.shape[0]
    s = lax.dot_general(qs, kx, (((1,), (1,)), ((), ())), preferred_element_type=F32)
    s = jnp.where(valid[None], s.reshape(n_rep, r // n_rep, nk), NEG).reshape(r, nk)
    m = jnp.maximum(jnp.broadcast_to(jnp.max(s, axis=-1, keepdims=True), sk.shape), sk)
    p = jnp.exp2(s - jnp.concatenate([m] * (nk // V7X_LANES), axis=1)).astype(BF16)
    ol = jnp.dot(p, vx_ones, preferred_element_type=F32)
    return ol[:, :V7X_LANES] / (ol[:, V7X_LANES:] + jnp.exp2(sk - m))


def _attend_all_heads(q_ref, o_ref, sink_ref, k_of_pair, v_of_pair, valid):
    rows = q_ref.shape[0]
    nk = valid.shape[1]
    lane = lax.broadcasted_iota(jnp.int32, (nk, V7X_LANES), 1)
    ones = jnp.ones((nk, V7X_LANES), BF16)
    scale = HEAD_DIM ** -0.5 * LOG2_E
    for pair in range(N_KV_PAIRS):
        kcol = k_of_pair(pair) * scale
        vcol = v_of_pair(pair)
        c0 = pair * CHUNKS_PER_PAIR
        qs = jnp.concatenate([q_ref[:, (c0 + g) * V7X_LANES:(c0 + g + 1) * V7X_LANES]
                              for g in range(CHUNKS_PER_PAIR)], axis=0)
        o_pair = None
        for parity in range(2):
            half = (lane < HEAD_DIM) if parity == 0 else (lane >= HEAD_DIM)
            kx = jnp.where(half, kcol, 0.0).astype(BF16)
            vx = jnp.concatenate([jnp.where(half, vcol, 0.0).astype(BF16), ones], axis=1)
            o = _attend(qs, kx, vx, valid, sink_ref[2 * pair + parity], CHUNKS_PER_PAIR)
            o_pair = o if o_pair is None else o_pair + o
        for g in range(CHUNKS_PER_PAIR):
            o_ref[:, (c0 + g) * V7X_LANES:(c0 + g + 1) * V7X_LANES] = (
                o_pair[g * rows:(g + 1) * rows].astype(o_ref.dtype))


def _sink_table(sinks, rows):
    per_row = jnp.repeat(sinks.astype(F32).reshape(N_KV_HEADS, N_GROUP) * LOG2_E, rows, axis=1)
    return jnp.broadcast_to(per_row[:, :, None], (N_KV_HEADS, N_GROUP * rows, V7X_LANES))


def _attn_prompt_kernel(sink_ref, q_ref, kc_ref, kp_ref, vc_ref, vp_ref, o_ref):
    n = pl.program_id(0)
    blk = q_ref.shape[0]
    row = lax.broadcasted_iota(jnp.int32, (blk, 2 * blk), 0)
    col = lax.broadcasted_iota(jnp.int32, (blk, 2 * blk), 1)
    diff = row + blk - col
    valid = (diff >= 0) & (diff < WINDOW) & ((n > 0) | (col >= blk))

    def both_blocks(prev_ref, cur_ref):
        return lambda pair: jnp.concatenate(
            [prev_ref[:, pair * V7X_LANES:(pair + 1) * V7X_LANES],
             cur_ref[:, pair * V7X_LANES:(pair + 1) * V7X_LANES]], axis=0)

    _attend_all_heads(q_ref, o_ref, sink_ref, both_blocks(kp_ref, kc_ref), both_blocks(vp_ref, vc_ref), valid)


def _attn_prompt(q, k, v, sinks, t_prompt):
    blk = WINDOW
    nb = t_prompt // blk
    cur = lambda n: (n, 0)
    prev = lambda n: (jnp.maximum(n - 1, 0), 0)
    return pl.pallas_call(
        _attn_prompt_kernel,
        grid=(nb,),
        in_specs=[pl.BlockSpec((N_KV_HEADS, N_GROUP * blk, V7X_LANES), lambda n: (0, 0, 0)),
                  pl.BlockSpec((blk, Q_DIM), cur),
                  pl.BlockSpec((blk, KV_DIM), cur),
                  pl.BlockSpec((blk, KV_DIM), prev),
                  pl.BlockSpec((blk, KV_DIM), cur),
                  pl.BlockSpec((blk, KV_DIM), prev)],
        out_specs=pl.BlockSpec((blk, Q_DIM), cur),
        out_shape=jax.ShapeDtypeStruct((t_prompt, Q_DIM), BF16),
        compiler_params=_params("arbitrary"),
        name="attn_prompt",
    )(_sink_table(sinks, blk), q, k, k, v, v)


SAMPLE_BATCH_BLOCK = 8
NEW_KEY_PAD = 128


def _attn_sample_kernel(sink_ref, q_ref, kn_ref, vn_ref, ck_ref, cv_ref, o_ref, ko_ref, vo_ref):
    bb, n_past = ck_ref.shape[0], ck_ref.shape[1]
    rows = q_ref.shape[0]
    t_new = rows // bb
    n_cache = bb * n_past
    nk = n_cache + NEW_KEY_PAD
    lg_t, lg_p = t_new.bit_length() - 1, n_past.bit_length() - 1

    for b in range(bb):
        ko_ref[b, 0:n_past - t_new, :] = ck_ref[b, t_new:n_past, :]
        ko_ref[b, n_past - t_new:n_past, :] = kn_ref[b * t_new:(b + 1) * t_new, :]
        vo_ref[b, 0:n_past - t_new, :] = cv_ref[b, t_new:n_past, :]
        vo_ref[b, n_past - t_new:n_past, :] = vn_ref[b * t_new:(b + 1) * t_new, :]

    r = lax.broadcasted_iota(jnp.int32, (rows, nk), 0)
    c = lax.broadcasted_iota(jnp.int32, (rows, nk), 1)
    b_r, t = r >> lg_t, r & (t_new - 1)
    d_cache = t + n_past - (c & (n_past - 1))
    valid_cache = (c < n_cache) & ((c >> lg_p) == b_r) & (d_cache >= 0) & (d_cache < WINDOW)
    cn = c - n_cache
    d_new = t - (cn & (t_new - 1))
    valid_new = (cn >= 0) & (cn < rows) & ((cn >> lg_t) == b_r) & (d_new >= 0) & (d_new < WINDOW)
    valid = valid_cache | valid_new

    pad = jnp.zeros((NEW_KEY_PAD - rows, V7X_LANES), F32)

    def keys(cache_ref, new_ref):
        def of_pair(pair):
            ls = slice(pair * V7X_LANES, (pair + 1) * V7X_LANES)
            return jnp.concatenate([cache_ref[:, :, ls].reshape(n_cache, V7X_LANES), new_ref[:, ls], pad], axis=0)
        return of_pair

    _attend_all_heads(q_ref, o_ref, sink_ref, keys(ck_ref, kn_ref), keys(cv_ref, vn_ref), valid)


def _attn_sample(q, k, v, cache_k, cache_v, sinks, row0, t_new):
    b, n_past, _ = cache_k.shape
    assert n_past == WINDOW and t_new & (t_new - 1) == 0
    bb = SAMPLE_BATCH_BLOCK
    rows = bb * t_new
    assert row0 % rows == 0 and rows <= NEW_KEY_PAD
    tok = lambda i: (row0 // rows + i, 0)
    i3 = lambda i: (i, 0, 0)
    return pl.pallas_call(
        _attn_sample_kernel,
        grid=(b // bb,),
        in_specs=[pl.BlockSpec((N_KV_HEADS, N_GROUP * rows, V7X_LANES), lambda i: (0, 0, 0)),
                  pl.BlockSpec((rows, Q_DIM), tok),
                  pl.BlockSpec((rows, KV_DIM), tok),
                  pl.BlockSpec((rows, KV_DIM), tok),
                  pl.BlockSpec((bb, n_past, KV_DIM), i3),
                  pl.BlockSpec((bb, n_past, KV_DIM), i3)],
        out_specs=[pl.BlockSpec((rows, Q_DIM), lambda i: (i, 0)),
                   pl.BlockSpec((bb, n_past, KV_DIM), i3),
                   pl.BlockSpec((bb, n_past, KV_DIM), i3)],
        out_shape=[jax.ShapeDtypeStruct((b * t_new, Q_DIM), BF16),
                   jax.ShapeDtypeStruct((b, n_past, KV_DIM), F32),
                   jax.ShapeDtypeStruct((b, n_past, KV_DIM), F32)],
        compiler_params=_params("arbitrary"),
        name="attn_sample",
    )(_sink_table(sinks, rows), q, k, v, cache_k, cache_v)


def _ln_swish(c, g, b):
    mu = jnp.mean(c, axis=-1, keepdims=True)
    var = jnp.mean(jnp.square(c - mu), axis=-1, keepdims=True)
    y = (c - mu) * lax.rsqrt(var + EPS) * g + b
    return jax.nn.silu(y)


CONV_HALO = 32
SUBLANES = 8
CONV_ROW_TILE = 128


def _conv_prompt_kernel(cur_ref, halo_ref, w_ref, b_ref, g_ref, bl_ref, side_ref, o_ref, side_o_ref,
                        xs_ref, c_ref):
    i = pl.program_id(0)
    tt = cur_ref.shape[0]
    side_o_ref[...] = side_ref[...].astype(side_o_ref.dtype)
    xs_ref[0:CONV_HALO, :] = jnp.where(i > 0, halo_ref[...], 0.0)
    xs_ref[CONV_HALO:CONV_HALO + tt, :] = cur_ref[...]
    off = CONV_HALO - (CONV_WIDTH - 1)
    n_groups = CONV_ROW_TILE // SUBLANES
    taps_by_shift = [[(w, (off + w) // SUBLANES) for w in range(CONV_WIDTH) if (off + w) % SUBLANES == s]
                     for s in range(SUBLANES)]
    n_in = max(n_groups + max(d for _, d in taps) + (s > 0) for s, taps in enumerate(taps_by_shift))
    assert tt - CONV_ROW_TILE + n_in * SUBLANES <= CONV_HALO + tt
    sub = lax.broadcasted_iota(jnp.int32, (SUBLANES, V7X_LANES), 0)

    def lane_chunk(cc, carry):
        ls = pl.ds(pl.multiple_of(cc * V7X_LANES, V7X_LANES), V7X_LANES)
        for r0 in range(0, tt, CONV_ROW_TILE):
            x0 = [xs_ref[pl.ds(r0 + kg * SUBLANES, SUBLANES), ls] for kg in range(n_in)]
            acc = [None] * n_groups
            for s, taps in enumerate(taps_by_shift):
                wv = {w: jnp.broadcast_to(w_ref[w:w + 1, ls], (SUBLANES, V7X_LANES)) for w, _ in taps}
                n_used = n_groups + max(d for _, d in taps)
                if s == 0:
                    xsft = x0
                else:
                    rot = [pltpu.roll(x0[kg], SUBLANES - s, axis=0) for kg in range(n_used + 1)]
                    xsft = [jnp.where(sub < SUBLANES - s, rot[kg], rot[kg + 1]) for kg in range(n_used)]
                for kg in range(n_used):
                    for w, d in taps:
                        og = kg - d
                        if 0 <= og < n_groups:
                            term = xsft[kg] * wv[w]
                            acc[og] = term if acc[og] is None else acc[og] + term
            bias = b_ref[:, ls]
            for og in range(n_groups):
                c_ref[pl.ds(r0 + og * SUBLANES, SUBLANES), ls] = acc[og] + bias
        return carry

    lax.fori_loop(0, cur_ref.shape[1] // V7X_LANES, lane_chunk, 0)
    o_ref[...] = _ln_swish(c_ref[...], g_ref[...], bl_ref[...]).astype(o_ref.dtype)


def _conv_prompt(cu, w_dw, b_dw, g_ln, b_ln, t_prompt, make_side_cast):
    tt = 256
    c = cu.shape[1]
    per = tt // CONV_HALO
    row1 = lambda i: (0, 0)
    side = make_side_cast(t_prompt // tt)
    return pl.pallas_call(
        _conv_prompt_kernel,
        grid=(t_prompt // tt,),
        in_specs=[pl.BlockSpec((tt, c), lambda i: (i, 0)),
                  pl.BlockSpec((CONV_HALO, c), lambda i: (jnp.maximum(i * per - 1, 0), 0)),
                  pl.BlockSpec((CONV_WIDTH, c), row1),
                  pl.BlockSpec((1, c), row1),
                  pl.BlockSpec((1, c), row1),
                  pl.BlockSpec((1, c), row1),
                  pl.BlockSpec(side.slab, lambda i: side.in_map(0, i))],
        out_specs=[pl.BlockSpec((tt, c), lambda i: (i, 0)),
                   pl.BlockSpec(side.slab, lambda i: side.out_map(0, i))],
        out_shape=[jax.ShapeDtypeStruct((t_prompt, c), BF16),
                   jax.ShapeDtypeStruct(side.out_shape, BF16)],
        scratch_shapes=[pltpu.VMEM((CONV_HALO + tt, c), F32), pltpu.VMEM((tt, c), F32)],
        compiler_params=_params("arbitrary"),
        name="conv_prompt",
    )(cu, cu, w_dw, b_dw.reshape(1, c), g_ln.reshape(1, c), b_ln.reshape(1, c), side.array)


CONV_SAMPLE_BATCH_TILE = 32


def _conv_sample_kernel(st_ref, cu_ref, w_ref, b_ref, g_ref, bl_ref, o_ref, so_ref, c_ref):
    n_hist, bt, c = st_ref.shape
    t_new = cu_ref.shape[0]

    def lane_chunk(cc, carry):
        ls = pl.ds(pl.multiple_of(cc * V7X_LANES, V7X_LANES), V7X_LANES)
        bias = b_ref[:, ls]
        for r0 in range(0, bt, SUBLANES):
            rs = slice(r0, r0 + SUBLANES)
            xs = [st_ref[r, rs, ls] for r in range(n_hist)] + [cu_ref[t, rs, ls] for t in range(t_new)]
            for t in range(t_new):
                acc = xs[t] * w_ref[0:1, ls]
                for w in range(1, CONV_WIDTH):
                    acc = acc + xs[t + w] * w_ref[w:w + 1, ls]
                c_ref[t, rs, ls] = acc + bias
            for r in range(n_hist):
                so_ref[r, rs, ls] = xs[r + t_new]
        return carry

    lax.fori_loop(0, c // V7X_LANES, lane_chunk, 0)
    for t in range(t_new):
        o_ref[t] = _ln_swish(c_ref[t], g_ref[...], bl_ref[...]).astype(o_ref.dtype)


def _conv_sample(cu_t, state_t, w_dw, b_dw, g_ln, b_ln):
    n_hist, b, c = state_t.shape
    t_new = cu_t.shape[0]
    assert n_hist == CONV_WIDTH - 1
    bt = CONV_SAMPLE_BATCH_TILE
    blk = lambda i: (0, i, 0)
    row1 = lambda i: (0, 0)
    return pl.pallas_call(
        _conv_sample_kernel,
        grid=(b // bt,),
        in_specs=[pl.BlockSpec((n_hist, bt, c), blk),
                  pl.BlockSpec((t_new, bt, c), blk),
                  pl.BlockSpec((CONV_WIDTH, c), row1),
                  pl.BlockSpec((1, c), row1),
                  pl.BlockSpec((1, c), row1),
                  pl.BlockSpec((1, c), row1)],
        out_specs=[pl.BlockSpec((t_new, bt, c), blk),
                   pl.BlockSpec((n_hist, bt, c), blk)],
        out_shape=[jax.ShapeDtypeStruct((t_new, b, c), BF16),
                   jax.ShapeDtypeStruct((n_hist, b, c), F32)],
        scratch_shapes=[pltpu.VMEM((t_new, bt, c), F32)],
        compiler_params=_params("arbitrary"),
        name="conv_sample",
    )(state_t, cu_t, w_dw, b_dw.reshape(1, c), g_ln.reshape(1, c), b_ln.reshape(1, c))


def _rope_tables(pos):
    half = HEAD_DIM // 2
    inv_freq = ROPE_THETA ** (-jnp.arange(half, dtype=F32) / half)
    ang = pos.astype(F32)[:, None] * inv_freq[None, :]
    cos, sin = jnp.cos(ang), jnp.sin(ang)
    cos64 = jnp.concatenate([cos, cos], axis=-1)
    sin64 = jnp.concatenate([-sin, sin], axis=-1)
    reps = V7X_LANES // HEAD_DIM
    return jnp.tile(cos64, (1, reps)), jnp.tile(sin64, (1, reps))


BIG_TOKEN_TILE = 1088
FFN_TOKEN_TILE = 2176
FFN_BLOCK = 256


def kernel(x_prompt, x_sample, cache_k, cache_v, state_conv, g_mix_norm, w_in, sinks, w_attn_o,
           w_dw, b_dw, g_conv_ln, b_conv_ln, w_conv_o, w_out, g_ffn_norm, w_ffn_in, w_ffn_out,
           g_final):
    depth = w_in.shape[0]
    assert depth == 1, "single-layer trunk"
    t_p = x_prompt.shape[1]
    b_s, t_s, _ = x_sample.shape
    m = t_p + b_s * t_s
    tm = TOKEN_TILE
    bw = WEIGHT_BLOCK
    sds = jax.ShapeDtypeStruct
    plain = _cast_plain

    x = Stacked(x_prompt.reshape(t_p, D_MODEL), x_sample.reshape(b_s * t_s, D_MODEL))
    pos = jnp.concatenate([jnp.arange(t_p, dtype=jnp.int32),
                           jnp.tile(PAST_LEN + jnp.arange(t_s, dtype=jnp.int32), b_s)])
    cos_t, sin_t = _rope_tables(pos)
    rope_extras = [(cos_t, (tm, V7X_LANES), lambda j, i: (i, 0)),
                   (sin_t, (tm, V7X_LANES), lambda j, i: (i, 0))]

    l = 0
    w_in_l = w_in[l]
    h = _rmsnorm(x, g_mix_norm[l], BF16, "rmsnorm_mix")

    kv_blk0 = Q_DIM // bw
    ua_blk0 = (Q_DIM + 2 * KV_DIM) // bw
    ub_blk0 = ua_blk0 + CONV_DIM // bw
    ga_blk0 = ub_blk0 + CONV_DIM // bw
    assert 2 * KV_DIM == bw and N_GROUP * HEAD_DIM == bw

    (q,) = _wres_matmul(
        [(h, w_in_l, bw, [lambda j: 2 * j, lambda j: 2 * j + 1], _cast_interleave_heads)],
        rope_extras,
        [(sds((m, Q_DIM), BF16), (tm, 2 * bw), lambda j, i: (i, j))],
        _epi_q, n_col_steps=N_KV_PAIRS, name="proj_q")

    k_all, v_all = _wres_matmul(
        [(h, w_in_l, bw, [lambda j: kv_blk0], plain)],
        rope_extras,
        [(sds((m, KV_DIM), F32), (tm, KV_DIM), lambda j, i: (i, 0)),
         (sds((m, KV_DIM), F32), (tm, KV_DIM), lambda j, i: (i, 0))],
        _epi_kv, n_col_steps=1, name="proj_kv")

    (cu,) = _wres_matmul(
        [(h, w_in_l, bw, [lambda j: ua_blk0 + j, lambda j: ub_blk0 + j], plain)],
        [],
        [(sds((m, CONV_DIM), F32), (tm, bw), lambda j, i: (i, j))],
        _epi_glu, n_col_steps=CONV_DIM // bw, name="proj_glu")

    n_past = cache_k.shape[2]
    o_p = _attn_prompt(q, k_all, v_all, sinks[l], t_p)
    o_s, k_win, v_win = _attn_sample(q, k_all, v_all,
                                     cache_k[l].reshape(b_s, n_past, KV_DIM),
                                     cache_v[l].reshape(b_s, n_past, KV_DIM), sinks[l], t_p, t_s)

    c_p, w_gates_bf = _conv_prompt(
        cu, w_dw[l], b_dw[l], g_conv_ln[l], b_conv_ln[l], t_p,
        lambda n_steps: _slab_stream(w_in_l, (D_MODEL // 2, bw), 1, n_steps, col0=ga_blk0 * bw))
    cu_s_t = cu[t_p:].reshape(b_s, t_s, CONV_DIM).transpose(1, 0, 2)
    c_s_t, conv_state_t = _conv_sample(cu_s_t, state_conv[l].transpose(1, 0, 2),
                                       w_dw[l], b_dw[l], g_conv_ln[l], b_conv_ln[l])
    c_s = c_s_t.transpose(1, 0, 2).reshape(b_s * t_s, CONV_DIM)

    btm = BIG_TOKEN_TILE
    gate_tn = 2 * bw
    n_gate_steps = 2 * D_MODEL // gate_tn
    gate_grid = (n_gate_steps, m // btm)
    gates, w_attn_o_bf, w_conv_o_bf, w_out_bf = _wres_matmul(
        [(h, w_gates_bf, gate_tn, [lambda j: j], None)],
        [],
        [(sds((m, 2 * D_MODEL), BF16), (btm, gate_tn), lambda j, i: (i, j))],
        _epi_sigmoid, n_col_steps=n_gate_steps, name="proj_gates", tm=btm,
        side_casts=[_slab_stream(w_attn_o[l], (HEAD_DIM, D_MODEL), *gate_grid, row_of_slab=_interleaved_head),
                    _slab_stream(w_conv_o[l], (CONV_DIM // 16, D_MODEL // 4), *gate_grid),
                    _slab_stream(w_out[l], (D_MODEL // 16, D_MODEL // 4), *gate_grid)])

    n_gate_blk = D_MODEL // (2 * bw)
    one_block = [lambda j: j]
    ffn_slab = (D_MODEL // 32, D_FF // 2)
    mix, w_fg_bf = _wres_matmul(
        [(Stacked(o_p, o_s), w_attn_o_bf, 2 * bw, one_block, None),
         (Stacked(c_p, c_s), w_conv_o_bf, 2 * bw, one_block, None)],
        [(gates, (tm, 2 * bw), lambda j, i: (i, j)),
         (gates, (tm, 2 * bw), lambda j, i: (i, j + n_gate_blk))],
        [(sds((m, D_MODEL), BF16), (tm, 2 * bw), lambda j, i: (i, j))],
        _epi_merge, n_col_steps=n_gate_blk, name="merge",
        side_casts=[_slab_stream(w_ffn_in[l], ffn_slab, n_gate_blk, m // tm, col0=0, n_cols=D_FF)])

    x1, w_fu_bf = _wres_matmul(
        [(mix, w_out_bf, 2 * bw, one_block, None)],
        [(x, (tm, 2 * bw), lambda j, i: (i, j))],
        [(sds((m, D_MODEL), F32), (tm, 2 * bw), lambda j, i: (i, j))],
        _epi_residual, n_col_steps=D_MODEL // (2 * bw), name="out_proj",
        side_casts=[_slab_stream(w_ffn_in[l], ffn_slab, D_MODEL // (2 * bw), m // tm, col0=D_FF, n_cols=D_FF)])

    h2 = _rmsnorm(x1, g_ffn_norm[l], BF16, "rmsnorm_ffn")
    ftm = FFN_TOKEN_TILE
    n_ff_blk = D_FF // FFN_BLOCK
    n_ff_steps = n_ff_blk * (m // ftm)
    act, w_ffn_out_bf = _wres_matmul(
        [(h2, (w_fg_bf, w_fu_bf), FFN_BLOCK, [lambda j: j, lambda j: j], None)],
        [],
        [(sds((m, D_FF), BF16), (ftm, FFN_BLOCK), lambda j, i: (i, j))],
        _epi_swiglu, n_col_steps=n_ff_blk, name="ffn_in", tm=ftm,
        side_casts=[_slab_stream(w_ffn_out[l], (D_FF // n_ff_steps, D_MODEL), n_ff_blk, m // ftm)])

    x2 = _ares_matmul_residual(act, w_ffn_out_bf, x1, tm=tm, tn=512, name="ffn_out")
    norm_tile = NORM_TILE
    y_p = _rmsnorm(x2, g_final, F32, "rmsnorm_final_prompt", row_tile0=0, n_tiles=t_p // norm_tile,
                   tm=norm_tile)
    y_s = _rmsnorm(x2, g_final, F32, "rmsnorm_final_sample", row_tile0=t_p // norm_tile,
                   n_tiles=(m - t_p) // norm_tile, tm=norm_tile)

    y_prompt = y_p.reshape(x_prompt.shape)
    y_sample = y_s.reshape(x_sample.shape)
    k_prompt = k_all[t_p - WINDOW:t_p].reshape(1, 1, WINDOW, N_KV_HEADS, HEAD_DIM)
    v_prompt = v_all[t_p - WINDOW:t_p].reshape(1, 1, WINDOW, N_KV_HEADS, HEAD_DIM)
    conv_prompt = cu[t_p - (CONV_WIDTH - 1):t_p].reshape(1, 1, CONV_WIDTH - 1, CONV_DIM)
    k_sample = k_win.reshape(1, b_s, WINDOW, N_KV_HEADS, HEAD_DIM)
    v_sample = v_win.reshape(1, b_s, WINDOW, N_KV_HEADS, HEAD_DIM)
    conv_sample = conv_state_t.transpose(1, 0, 2).reshape(1, b_s, CONV_WIDTH - 1, CONV_DIM)
    return (y_prompt, y_sample, k_prompt, v_prompt, conv_prompt, k_sample, v_sample, conv_sample)
```

```python
import collections
import functools

import jax
import jax.numpy as jnp
from jax import lax
from jax.experimental import pallas as pl
from jax.experimental.pallas import tpu as pltpu

D_MODEL = 4096
HEAD_DIM = 64
N_HEADS = 32
N_KV_HEADS = 4
N_GROUP = N_HEADS // N_KV_HEADS
WINDOW = 128
ROPE_THETA = 10000.0
CONV_DIM = 2048
CONV_WIDTH = 31
D_FF = 11008
PAST_LEN = 8192
Q_DIM = N_HEADS * HEAD_DIM
KV_DIM = N_KV_HEADS * HEAD_DIM
EPS = 1e-6
NEG = -1e30
LOG2_E = 1.4426950408889634

V7X_LANES = 128
V7X_VMEM_LIMIT_BYTES = 60000 * 1024

F32 = jnp.float32
BF16 = jnp.bfloat16

TOKEN_TILE = 512
WEIGHT_BLOCK = 512
CAST_ROWS = 256
N_KV_PAIRS = N_KV_HEADS // 2
CHUNKS_PER_PAIR = N_GROUP

Stacked = collections.namedtuple("Stacked", ["prompt", "sample"])

SideCast = collections.namedtuple("SideCast", ["array", "slab", "in_map", "out_shape", "out_map"])


def _slab_stream(array, slab, n_outer, n_inner, *, col0=0, n_cols=None, row_of_slab=None):
    k, n = array.shape
    n_cols = n - col0 if n_cols is None else n_cols
    assert k % slab[0] == 0 and n_cols % slab[1] == 0 and col0 % slab[1] == 0
    n_r, n_c = k // slab[0], n_cols // slab[1]
    assert n_r * n_c <= n_outer * n_inner, "not enough grid steps to cast every slab"

    def rc(j, i):
        t = jnp.minimum(j * n_inner + i, n_r * n_c - 1)
        return t // n_c, t % n_c

    def in_map(j, i):
        r, c = rc(j, i)
        return (r if row_of_slab is None else row_of_slab(r), col0 // slab[1] + c)

    return SideCast(array, slab, in_map, (k, n_cols), rc)


def _params(*sem):
    return pltpu.CompilerParams(dimension_semantics=sem,
                                vmem_limit_bytes=V7X_VMEM_LIMIT_BYTES)


def _stacked_specs(op, block, imap, tm):
    if not isinstance(op, Stacked):
        return [op], [pl.BlockSpec(block, imap)]
    n_p = op.prompt.shape[0] // tm
    assert op.prompt.shape[0] % tm == 0 and op.sample.shape[0] % tm == 0

    def p_map(*ids):
        r, c = imap(*ids)
        return (jnp.minimum(r, n_p - 1), c)

    def s_map(*ids):
        r, c = imap(*ids)
        return (jnp.maximum(r - n_p, 0), c)

    return [op.prompt, op.sample], [pl.BlockSpec(block, p_map), pl.BlockSpec(block, s_map)]


def _n_prompt_tiles(op, tm):
    return op.prompt.shape[0] // tm if isinstance(op, Stacked) else None


def _load(refs, n_p, row_tile):
    if n_p is None:
        return refs[0][...]
    return jnp.where(row_tile < n_p, refs[0][...], refs[1][...])


def _rmsnorm_kernel(*refs, n_p):
    x_refs, (g_ref, o_ref) = refs[:-2], refs[-2:]
    x = _load(x_refs, n_p, pl.program_id(0))
    y = x * lax.rsqrt(jnp.mean(x * x, axis=-1, keepdims=True) + EPS) * g_ref[...]
    o_ref[...] = y.astype(o_ref.dtype)


NORM_TILE = 512


def _rmsnorm(x, g, out_dtype, name, *, row_tile0=0, n_tiles=None, tm=NORM_TILE):
    d = g.shape[0]
    rows = (x.prompt.shape[0] + x.sample.shape[0]) if isinstance(x, Stacked) else x.shape[0]
    n_tiles = rows // tm if n_tiles is None else n_tiles
    arrays, specs = _stacked_specs(x, (tm, d), lambda i: (i + row_tile0, 0), tm)
    return pl.pallas_call(
        functools.partial(_rmsnorm_kernel, n_p=_n_prompt_tiles(x, tm)),
        grid=(n_tiles,),
        in_specs=specs + [pl.BlockSpec((1, d), lambda i: (0, 0))],
        out_specs=pl.BlockSpec((tm, d), lambda i: (i, 0)),
        out_shape=jax.ShapeDtypeStruct((n_tiles * tm, d), out_dtype),
        compiler_params=_params("arbitrary"),
        name=name,
    )(*arrays, g.reshape(1, d))


def _cast_plain(w_refs, wbf_ref):
    k, bw = w_refs[0].shape

    def body(c, carry):
        r = pl.multiple_of(c * CAST_ROWS, CAST_ROWS)
        for b, w_ref in enumerate(w_refs):
            wbf_ref[pl.ds(r, CAST_ROWS), b * bw:(b + 1) * bw] = w_ref[pl.ds(r, CAST_ROWS), :].astype(BF16)
        return carry

    lax.fori_loop(0, k // CAST_ROWS, body, 0)


def _cast_interleave_heads(w_refs, wbf_ref):
    w_even, w_odd = w_refs
    k, bw = w_even.shape

    def body(c, carry):
        r = pl.multiple_of(c * CAST_ROWS, CAST_ROWS)
        a = w_even[pl.ds(r, CAST_ROWS), :]
        b = w_odd[pl.ds(r, CAST_ROWS), :]
        pieces = []
        for g in range(bw // HEAD_DIM):
            pieces += [a[:, g * HEAD_DIM:(g + 1) * HEAD_DIM], b[:, g * HEAD_DIM:(g + 1) * HEAD_DIM]]
        wbf_ref[pl.ds(r, CAST_ROWS), :] = jnp.concatenate(pieces, axis=1).astype(BF16)
        return carry

    lax.fori_loop(0, k // CAST_ROWS, body, 0)


def _interleaved_head(slot):
    chunk, parity = divmod(slot, 2)
    pair, g = divmod(chunk, CHUNKS_PER_PAIR)
    return (2 * pair + parity) * N_GROUP + g


def _wres_kernel(*refs, layout, casts, epilogue):
    a_counts, w_counts, extra_counts, n_out, n_side, a_np, extra_np = layout
    pos = 0

    def take(counts):
        nonlocal pos
        groups = []
        for n in counts:
            groups.append(refs[pos:pos + n])
            pos += n
        return groups

    a_groups = take(a_counts)
    w_groups = take(w_counts)
    extra_groups = take(extra_counts)
    side_in = refs[pos:pos + n_side]
    pos += n_side
    out_refs = refs[pos:pos + n_out]
    pos += n_out
    side_out = refs[pos:pos + n_side]
    scratch_refs = list(refs[pos + n_side:])
    wbf_refs = [scratch_refs.pop(0) if cast is not None else None for cast in casts]
    i = pl.program_id(1)

    if any(cast is not None for cast in casts):
        @pl.when(i == 0)
        def _():
            for cast, w_refs, wbf_ref in zip(casts, w_groups, wbf_refs):
                if cast is not None:
                    cast(w_refs, wbf_ref)

    for s_in, s_out in zip(side_in, side_out):
        s_out[...] = s_in[...].astype(s_out.dtype)

    accs = []
    for a_refs, n_p, w_refs, wbf_ref in zip(a_groups, a_np, w_groups, wbf_refs):
        a = _load(a_refs, n_p, i)
        if wbf_ref is not None:
            accs.append(jnp.dot(a, wbf_ref[...], preferred_element_type=F32))
        else:
            parts = [jnp.dot(a, w_ref[...], preferred_element_type=F32) for w_ref in w_refs]
            accs.append(parts[0] if len(parts) == 1 else jnp.concatenate(parts, axis=1))
    extras = [_load(e_refs, n_p, i) for e_refs, n_p in zip(extra_groups, extra_np)]
    for o_ref, o in zip(out_refs, epilogue(accs, extras)):
        o_ref[...] = o.astype(o_ref.dtype)


def _wres_matmul(terms, extras, outs, epilogue, *, n_col_steps, name, tm=TOKEN_TILE,
                 weight_buffers=2, side_casts=()):
    a0 = terms[0][0]
    m = (a0.prompt.shape[0] + a0.sample.shape[0]) if isinstance(a0, Stacked) else a0.shape[0]
    assert m % tm == 0
    in_arrays, in_specs, scratch, casts = [], [], [], []
    a_counts, w_counts, extra_counts, a_np, extra_np = [], [], [], [], []
    for a, w, bw, col_fns, cast in terms:
        k = (w[0] if isinstance(w, tuple) else w).shape[0]
        arrs, specs = _stacked_specs(a, (tm, k), lambda j, i: (i, 0), tm)
        in_arrays += arrs
        in_specs += specs
        a_counts.append(len(arrs))
        a_np.append(_n_prompt_tiles(a, tm))
    for a, w, bw, col_fns, cast in terms:
        w_of_block = w if isinstance(w, tuple) else (w,) * len(col_fns)
        k = w_of_block[0].shape[0]
        for fn, w_blk in zip(col_fns, w_of_block):
            in_arrays.append(w_blk)
            in_specs.append(pl.BlockSpec((k, bw), functools.partial(lambda j, i, fn: (0, fn(j)), fn=fn),
                                         pipeline_mode=pl.Buffered(weight_buffers)))
        w_counts.append(len(col_fns))
        casts.append(cast)
        if cast is not None:
            scratch.append(pltpu.VMEM((k, bw * len(col_fns)), BF16))
    for op, blk, imap in extras:
        arrs, specs = _stacked_specs(op, blk, imap, tm)
        in_arrays += arrs
        in_specs += specs
        extra_counts.append(len(arrs))
        extra_np.append(_n_prompt_tiles(op, tm))
    for sc in side_casts:
        in_arrays.append(sc.array)
        in_specs.append(pl.BlockSpec(sc.slab, sc.in_map))
    layout = (tuple(a_counts), tuple(w_counts), tuple(extra_counts), len(outs), len(side_casts),
              tuple(a_np), tuple(extra_np))
    return pl.pallas_call(
        functools.partial(_wres_kernel, layout=layout, casts=tuple(casts), epilogue=epilogue),
        grid=(n_col_steps, m // tm),
        in_specs=in_specs,
        out_specs=([pl.BlockSpec(blk, imap) for _, blk, imap in outs]
                   + [pl.BlockSpec(sc.slab, sc.out_map) for sc in side_casts]),
        out_shape=([s for s, _, _ in outs]
                   + [jax.ShapeDtypeStruct(sc.out_shape, BF16) for sc in side_casts]),
        scratch_shapes=scratch,
        compiler_params=_params("arbitrary", "arbitrary"),
        name=name,
    )(*in_arrays)


def _ares_kernel(a_ref, w_ref, x_ref, o_ref):
    acc = jnp.dot(a_ref[...], w_ref[...], preferred_element_type=F32)
    o_ref[...] = x_ref[...] + acc


def _ares_matmul_residual(a, w_bf, x, *, tm, tn, name):
    m, k = a.shape
    n = w_bf.shape[1]
    return pl.pallas_call(
        _ares_kernel,
        grid=(m // tm, n // tn),
        in_specs=[pl.BlockSpec((tm, k), lambda i, j: (i, 0)),
                  pl.BlockSpec((k, tn), lambda i, j: (0, j)),
                  pl.BlockSpec((tm, tn), lambda i, j: (i, j))],
        out_specs=pl.BlockSpec((tm, tn), lambda i, j: (i, j)),
        out_shape=jax.ShapeDtypeStruct((m, n), F32),
        compiler_params=_params("arbitrary", "arbitrary"),
        name=name,
    )(a, w_bf, x)


def _rope(x, cos128, sin128):
    n = x.shape[1]
    reps = n // V7X_LANES
    c = jnp.concatenate([cos128] * reps, axis=1) if reps > 1 else cos128
    s = jnp.concatenate([sin128] * reps, axis=1) if reps > 1 else sin128
    lane = lax.broadcasted_iota(jnp.int32, x.shape, 1)
    first_half = (lane & (HEAD_DIM - 1)) < (HEAD_DIM // 2)
    partner = jnp.where(first_half,
                        pltpu.roll(x, n - HEAD_DIM // 2, axis=1),
                        pltpu.roll(x, HEAD_DIM // 2, axis=1))
    return x * c + partner * s


def _sigmoid(x):
    return 0.5 * jnp.tanh(0.5 * x) + 0.5


def _epi_q(accs, extras):
    cos, sin = extras
    return [_rope(accs[0], cos, sin)]


def _epi_kv(accs, extras):
    cos, sin = extras
    acc = accs[0]
    return [_rope(acc[:, :KV_DIM], cos, sin), acc[:, KV_DIM:]]


def _epi_glu(accs, extras):
    acc = accs[0]
    half = acc.shape[1] // 2
    return [acc[:, :half] * _sigmoid(acc[:, half:])]


def _epi_sigmoid(accs, extras):
    return [_sigmoid(accs[0])]


def _epi_swiglu(accs, extras):
    acc = accs[0]
    half = acc.shape[1] // 2
    return [acc[:, :half] * _sigmoid(acc[:, :half]) * acc[:, half:]]


def _epi_merge(accs, extras):
    g_a, g_c = extras
    return [g_a.astype(F32) * accs[0] + g_c.astype(F32) * accs[1]]


def _epi_residual(accs, extras):
    return [extras[0] + accs[0]]


def _attend(qs, kx, vx_ones, valid, sk, n_rep):
    r, nk = qs.shape[0], kx.shape[0]
    s = lax.dot_general(qs, kx, (((1,), (1,)), ((), ())), preferred_element_type=F32)
    s = jnp.where(valid[None], s.reshape(n_rep, r // n_rep, nk), NEG).reshape(r, nk)
    m = jnp.maximum(jnp.broadcast_to(jnp.max(s, axis=-1, keepdims=True), sk.shape), sk)
    p = jnp.exp2(s - jnp.concatenate([m] * (nk // V7X_LANES), axis=1)).astype(BF16)
    ol = jnp.dot(p, vx_ones, preferred_element_type=F32)
    return ol[:, :V7X_LANES] / (ol[:, V7X_LANES:] + jnp.exp2(sk - m))


def _attend_all_heads(q_ref, o_ref, sink_ref, k_of_pair, v_of_pair, valid):
    rows = q_ref.shape[0]
    nk = valid.shape[1]
    lane = lax.broadcasted_iota(jnp.int32, (nk, V7X_LANES), 1)
    ones = jnp.ones((nk, V7X_LANES), BF16)
    scale = HEAD_DIM ** -0.5 * LOG2_E
    for pair in range(N_KV_PAIRS):
        kcol = k_of_pair(pair) * scale
        vcol = v_of_pair(pair)
        c0 = pair * CHUNKS_PER_PAIR
        qs = jnp.concatenate([q_ref[:, (c0 + g) * V7X_LANES:(c0 + g + 1) * V7X_LANES]
                              for g in range(CHUNKS_PER_PAIR)], axis=0)
        o_pair = None
        for parity in range(2):
            half = (lane < HEAD_DIM) if parity == 0 else (lane >= HEAD_DIM)
            kx = jnp.where(half, kcol, 0.0).astype(BF16)
            vx = jnp.concatenate([jnp.where(half, vcol, 0.0).astype(BF16), ones], axis=1)
            o = _attend(qs, kx, vx, valid, sink_ref[2 * pair + parity], CHUNKS_PER_PAIR)
            o_pair = o if o_pair is None else o_pair + o
        for g in range(CHUNKS_PER_PAIR):
            o_ref[:, (c0 + g) * V7X_LANES:(c0 + g + 1) * V7X_LANES] = (
                o_pair[g * rows:(g + 1) * rows].astype(o_ref.dtype))


def _sink_table(sinks, rows):
    per_row = jnp.repeat(sinks.astype(F32).reshape(N_KV_HEADS, N_GROUP) * LOG2_E, rows, axis=1)
    return jnp.broadcast_to(per_row[:, :, None], (N_KV_HEADS, N_GROUP * rows, V7X_LANES))


def _attn_prompt_kernel(sink_ref, q_ref, kc_ref, kp_ref, vc_ref, vp_ref, o_ref):
    n = pl.program_id(0)
    blk = q_ref.shape[0]
    row = lax.broadcasted_iota(jnp.int32, (blk, 2 * blk), 0)
    col = lax.broadcasted_iota(jnp.int32, (blk, 2 * blk), 1)
    diff = row + blk - col
    valid = (diff >= 0) & (diff < WINDOW) & ((n > 0) | (col >= blk))

    def both_blocks(prev_ref, cur_ref):
        return lambda pair: jnp.concatenate(
            [prev_ref[:, pair * V7X_LANES:(pair + 1) * V7X_LANES],
             cur_ref[:, pair * V7X_LANES:(pair + 1) * V7X_LANES]], axis=0)

    _attend_all_heads(q_ref, o_ref, sink_ref, both_blocks(kp_ref, kc_ref), both_blocks(vp_ref, vc_ref), valid)


def _attn_prompt(q, k, v, sinks, t_prompt):
    blk = WINDOW
    nb = t_prompt // blk
    cur = lambda n: (n, 0)
    prev = lambda n: (jnp.maximum(n - 1, 0), 0)
    return pl.pallas_call(
        _attn_prompt_kernel,
        grid=(nb,),
        in_specs=[pl.BlockSpec((N_KV_HEADS, N_GROUP * blk, V7X_LANES), lambda n: (0, 0, 0)),
                  pl.BlockSpec((blk, Q_DIM), cur),
                  pl.BlockSpec((blk, KV_DIM), cur),
                  pl.BlockSpec((blk, KV_DIM), prev),
                  pl.BlockSpec((blk, KV_DIM), cur),
                  pl.BlockSpec((blk, KV_DIM), prev)],
        out_specs=pl.BlockSpec((blk, Q_DIM), cur),
        out_shape=jax.ShapeDtypeStruct((t_prompt, Q_DIM), BF16),
        compiler_params=_params("arbitrary"),
        name="attn_prompt",
    )(_sink_table(sinks, blk), q, k, k, v, v)


SAMPLE_BATCH_BLOCK = 4
NEW_KEY_PAD = 128


def _attn_sample_kernel(sink_ref, q_ref, kn_ref, vn_ref, ck_ref, cv_ref, o_ref, ko_ref, vo_ref):
    bb, n_past = ck_ref.shape[0], ck_ref.shape[1]
    rows = q_ref.shape[0]
    t_new = rows // bb
    n_cache = bb * n_past
    nk = n_cache + NEW_KEY_PAD
    lg_t, lg_p = t_new.bit_length() - 1, n_past.bit_length() - 1

    for b in range(bb):
        ko_ref[b, 0:n_past - t_new, :] = ck_ref[b, t_new:n_past, :]
        ko_ref[b, n_past - t_new:n_past, :] = kn_ref[b * t_new:(b + 1) * t_new, :]
        vo_ref[b, 0:n_past - t_new, :] = cv_ref[b, t_new:n_past, :]
        vo_ref[b, n_past - t_new:n_past, :] = vn_ref[b * t_new:(b + 1) * t_new, :]

    r = lax.broadcasted_iota(jnp.int32, (rows, nk), 0)
    c = lax.broadcasted_iota(jnp.int32, (rows, nk), 1)
    b_r, t = r >> lg_t, r & (t_new - 1)
    d_cache = t + n_past - (c & (n_past - 1))
    valid_cache = (c < n_cache) & ((c >> lg_p) == b_r) & (d_cache >= 0) & (d_cache < WINDOW)
    cn = c - n_cache
    d_new = t - (cn & (t_new - 1))
    valid_new = (cn >= 0) & (cn < rows) & ((cn >> lg_t) == b_r) & (d_new >= 0) & (d_new < WINDOW)
    valid = valid_cache | valid_new

    pad = jnp.zeros((NEW_KEY_PAD - rows, V7X_LANES), F32)

    def keys(cache_ref, new_ref):
        def of_pair(pair):
            ls = slice(pair * V7X_LANES, (pair + 1) * V7X_LANES)
            return jnp.concatenate([cache_ref[:, :, ls].reshape(n_cache, V7X_LANES), new_ref[:, ls], pad], axis=0)
        return of_pair

    _attend_all_heads(q_ref, o_ref, sink_ref, keys(ck_ref, kn_ref), keys(cv_ref, vn_ref), valid)


def _attn_sample(q, k, v, cache_k, cache_v, sinks, row0, t_new):
    b, n_past, _ = cache_k.shape
    assert n_past == WINDOW and t_new & (t_new - 1) == 0
    bb = SAMPLE_BATCH_BLOCK
    rows = bb * t_new
    assert row0 % rows == 0 and rows <= NEW_KEY_PAD
    tok = lambda i: (row0 // rows + i, 0)
    i3 = lambda i: (i, 0, 0)
    return pl.pallas_call(
        _attn_sample_kernel,
        grid=(b // bb,),
        in_specs=[pl.BlockSpec((N_KV_HEADS, N_GROUP * rows, V7X_LANES), lambda i: (0, 0, 0)),
                  pl.BlockSpec((rows, Q_DIM), tok),
                  pl.BlockSpec((rows, KV_DIM), tok),
                  pl.BlockSpec((rows, KV_DIM), tok),
                  pl.BlockSpec((bb, n_past, KV_DIM), i3),
                  pl.BlockSpec((bb, n_past, KV_DIM), i3)],
        out_specs=[pl.BlockSpec((rows, Q_DIM), lambda i: (i, 0)),
                   pl.BlockSpec((bb, n_past, KV_DIM), i3),
                   pl.BlockSpec((bb, n_past, KV_DIM), i3)],
        out_shape=[jax.ShapeDtypeStruct((b * t_new, Q_DIM), BF16),
                   jax.ShapeDtypeStruct((b, n_past, KV_DIM), F32),
                   jax.ShapeDtypeStruct((b, n_past, KV_DIM), F32)],
        compiler_params=_params("arbitrary"),
        name="attn_sample",
    )(_sink_table(sinks, rows), q, k, v, cache_k, cache_v)


def _ln_swish(c, g, b):
    mu = jnp.mean(c, axis=-1, keepdims=True)
    var = jnp.mean(jnp.square(c - mu), axis=-1, keepdims=True)
    y = (c - mu) * lax.rsqrt(var + EPS) * g + b
    return jax.nn.silu(y)


CONV_HALO = 32
SUBLANES = 8
CONV_ROW_TILE = 128


def _conv_prompt_kernel(cur_ref, halo_ref, w_ref, b_ref, g_ref, bl_ref, side_ref, o_ref, side_o_ref, c_ref):
    i = pl.program_id(0)
    tt = cur_ref.shape[0]
    side_o_ref[...] = side_ref[...].astype(side_o_ref.dtype)
    off = CONV_HALO - (CONV_WIDTH - 1)
    n_groups = CONV_ROW_TILE // SUBLANES
    taps_by_shift = [[(w, (off + w) // SUBLANES) for w in range(CONV_WIDTH) if (off + w) % SUBLANES == s]
                     for s in range(SUBLANES)]
    n_in = max(n_groups + max(d for _, d in taps) + (s > 0) for s, taps in enumerate(taps_by_shift))
    assert tt - CONV_ROW_TILE + n_in * SUBLANES <= CONV_HALO + tt and CONV_HALO % SUBLANES == 0
    sub = lax.broadcasted_iota(jnp.int32, (SUBLANES, V7X_LANES), 0)

    def input_group(row, ls):
        if row < CONV_HALO:
            return jnp.where(i > 0, halo_ref[pl.ds(row, SUBLANES), ls], 0.0)
        return cur_ref[pl.ds(row - CONV_HALO, SUBLANES), ls]

    def lane_chunk(cc, carry):
        ls = pl.ds(pl.multiple_of(cc * V7X_LANES, V7X_LANES), V7X_LANES)
        for r0 in range(0, tt, CONV_ROW_TILE):
            x0 = [input_group(r0 + kg * SUBLANES, ls) for kg in range(n_in)]
            acc = [None] * n_groups
            for s, taps in enumerate(taps_by_shift):
                wv = {w: jnp.broadcast_to(w_ref[w:w + 1, ls], (SUBLANES, V7X_LANES)) for w, _ in taps}
                n_used = n_groups + max(d for _, d in taps)
                if s == 0:
                    xsft = x0
                else:
                    rot = [pltpu.roll(x0[kg], SUBLANES - s, axis=0) for kg in range(n_used + 1)]
                    xsft = [jnp.where(sub < SUBLANES - s, rot[kg], rot[kg + 1]) for kg in range(n_used)]
                for kg in range(n_used):
                    for w, d in taps:
                        og = kg - d
                        if 0 <= og < n_groups:
                            term = xsft[kg] * wv[w]
                            acc[og] = term if acc[og] is None else acc[og] + term
            bias = b_ref[:, ls]
            for og in range(n_groups):
                c_ref[pl.ds(r0 + og * SUBLANES, SUBLANES), ls] = acc[og] + bias
        return carry

    lax.fori_loop(0, cur_ref.shape[1] // V7X_LANES, lane_chunk, 0)
    o_ref[...] = _ln_swish(c_ref[...], g_ref[...], bl_ref[...]).astype(o_ref.dtype)


def _conv_prompt(cu, w_dw, b_dw, g_ln, b_ln, t_prompt, make_side_cast):
    tt = 256
    c = cu.shape[1]
    per = tt // CONV_HALO
    row1 = lambda i: (0, 0)
    side = make_side_cast(t_prompt // tt)
    return pl.pallas_call(
        _conv_prompt_kernel,
        grid=(t_prompt // tt,),
        in_specs=[pl.BlockSpec((tt, c), lambda i: (i, 0)),
                  pl.BlockSpec((CONV_HALO, c), lambda i: (jnp.maximum(i * per - 1, 0), 0)),
                  pl.BlockSpec((CONV_WIDTH, c), row1),
                  pl.BlockSpec((1, c), row1),
                  pl.BlockSpec((1, c), row1),
                  pl.BlockSpec((1, c), row1),
                  pl.BlockSpec(side.slab, lambda i: side.in_map(0, i))],
        out_specs=[pl.BlockSpec((tt, c), lambda i: (i, 0)),
                   pl.BlockSpec(side.slab, lambda i: side.out_map(0, i))],
        out_shape=[jax.ShapeDtypeStruct((t_prompt, c), BF16),
                   jax.ShapeDtypeStruct(side.out_shape, BF16)],
        scratch_shapes=[pltpu.VMEM((tt, c), F32)],
        compiler_params=_params("arbitrary"),
        name="conv_prompt",
    )(cu, cu, w_dw, b_dw.reshape(1, c), g_ln.reshape(1, c), b_ln.reshape(1, c), side.array)


CONV_SAMPLE_BATCH_TILE = 32


def _conv_sample_kernel(st_ref, cu_ref, w_ref, b_ref, g_ref, bl_ref, o_ref, so_ref, c_ref):
    n_hist, bt, c = st_ref.shape
    t_new = cu_ref.shape[0]

    def lane_chunk(cc, carry):
        ls = pl.ds(pl.multiple_of(cc * V7X_LANES, V7X_LANES), V7X_LANES)
        bias = b_ref[:, ls]
        for r0 in range(0, bt, SUBLANES):
            rs = slice(r0, r0 + SUBLANES)
            xs = [st_ref[r, rs, ls] for r in range(n_hist)] + [cu_ref[t, rs, ls] for t in range(t_new)]
            for t in range(t_new):
                acc = xs[t] * w_ref[0:1, ls]
                for w in range(1, CONV_WIDTH):
                    acc = acc + xs[t + w] * w_ref[w:w + 1, ls]
                c_ref[t, rs, ls] = acc + bias
            for r in range(n_hist):
                so_ref[r, rs, ls] = xs[r + t_new]
        return carry

    lax.fori_loop(0, c // V7X_LANES, lane_chunk, 0)
    for t in range(t_new):
        o_ref[t] = _ln_swish(c_ref[t], g_ref[...], bl_ref[...]).astype(o_ref.dtype)


def _conv_sample(cu_t, state_t, w_dw, b_dw, g_ln, b_ln):
    n_hist, b, c = state_t.shape
    t_new = cu_t.shape[0]
    assert n_hist == CONV_WIDTH - 1
    bt = CONV_SAMPLE_BATCH_TILE
    blk = lambda i: (0, i, 0)
    row1 = lambda i: (0, 0)
    return pl.pallas_call(
        _conv_sample_kernel,
        grid=(b // bt,),
        in_specs=[pl.BlockSpec((n_hist, bt, c), blk),
                  pl.BlockSpec((t_new, bt, c), blk),
                  pl.BlockSpec((CONV_WIDTH, c), row1),
                  pl.BlockSpec((1, c), row1),
                  pl.BlockSpec((1, c), row1),
                  pl.BlockSpec((1, c), row1)],
        out_specs=[pl.BlockSpec((t_new, bt, c), blk),
                   pl.BlockSpec((n_hist, bt, c), blk)],
        out_shape=[jax.ShapeDtypeStruct((t_new, b, c), BF16),
                   jax.ShapeDtypeStruct((n_hist, b, c), F32)],
        scratch_shapes=[pltpu.VMEM((t_new, bt, c), F32)],
        compiler_params=_params("arbitrary"),
        name="conv_sample",
    )(state_t, cu_t, w_dw, b_dw.reshape(1, c), g_ln.reshape(1, c), b_ln.reshape(1, c))


def _rope_tables(pos):
    half = HEAD_DIM // 2
    inv_freq = ROPE_THETA ** (-jnp.arange(half, dtype=F32) / half)
    ang = pos.astype(F32)[:, None] * inv_freq[None, :]
    cos, sin = jnp.cos(ang), jnp.sin(ang)
    cos64 = jnp.concatenate([cos, cos], axis=-1)
    sin64 = jnp.concatenate([-sin, sin], axis=-1)
    reps = V7X_LANES // HEAD_DIM
    return jnp.tile(cos64, (1, reps)), jnp.tile(sin64, (1, reps))


BIG_TOKEN_TILE = 1088
FFN_TOKEN_TILE = 2176
FFN_BLOCK = 256


def kernel(x_prompt, x_sample, cache_k, cache_v, state_conv, g_mix_norm, w_in, sinks, w_attn_o,
           w_dw, b_dw, g_conv_ln, b_conv_ln, w_conv_o, w_out, g_ffn_norm, w_ffn_in, w_ffn_out,
           g_final):
    depth = w_in.shape[0]
    assert depth == 1, "single-layer trunk"
    t_p = x_prompt.shape[1]
    b_s, t_s, _ = x_sample.shape
    m = t_p + b_s * t_s
    tm = TOKEN_TILE
    bw = WEIGHT_BLOCK
    sds = jax.ShapeDtypeStruct
    plain = _cast_plain

    x = Stacked(x_prompt.reshape(t_p, D_MODEL), x_sample.reshape(b_s * t_s, D_MODEL))
    pos = jnp.concatenate([jnp.arange(t_p, dtype=jnp.int32),
                           jnp.tile(PAST_LEN + jnp.arange(t_s, dtype=jnp.int32), b_s)])
    cos_t, sin_t = _rope_tables(pos)
    rope_extras = [(cos_t, (tm, V7X_LANES), lambda j, i: (i, 0)),
                   (sin_t, (tm, V7X_LANES), lambda j, i: (i, 0))]

    l = 0
    w_in_l = w_in[l]
    h = _rmsnorm(x, g_mix_norm[l], BF16, "rmsnorm_mix")

    kv_blk0 = Q_DIM // bw
    ua_blk0 = (Q_DIM + 2 * KV_DIM) // bw
    ub_blk0 = ua_blk0 + CONV_DIM // bw
    ga_blk0 = ub_blk0 + CONV_DIM // bw
    assert 2 * KV_DIM == bw and N_GROUP * HEAD_DIM == bw

    (q,) = _wres_matmul(
        [(h, w_in_l, bw, [lambda j: 2 * j, lambda j: 2 * j + 1], _cast_interleave_heads)],
        rope_extras,
        [(sds((m, Q_DIM), BF16), (tm, 2 * bw), lambda j, i: (i, j))],
        _epi_q, n_col_steps=N_KV_PAIRS, name="proj_q")

    k_all, v_all = _wres_matmul(
        [(h, w_in_l, bw, [lambda j: kv_blk0], plain)],
        rope_extras,
        [(sds((m, KV_DIM), F32), (tm, KV_DIM), lambda j, i: (i, 0)),
         (sds((m, KV_DIM), F32), (tm, KV_DIM), lambda j, i: (i, 0))],
        _epi_kv, n_col_steps=1, name="proj_kv")

    (cu,) = _wres_matmul(
        [(h, w_in_l, bw, [lambda j: ua_blk0 + j, lambda j: ub_blk0 + j], plain)],
        [],
        [(sds((m, CONV_DIM), F32), (tm, bw), lambda j, i: (i, j))],
        _epi_glu, n_col_steps=CONV_DIM // bw, name="proj_glu")

    n_past = cache_k.shape[2]
    o_p = _attn_prompt(q, k_all, v_all, sinks[l], t_p)
    o_s, k_win, v_win = _attn_sample(q, k_all, v_all,
                                     cache_k[l].reshape(b_s, n_past, KV_DIM),
                                     cache_v[l].reshape(b_s, n_past, KV_DIM), sinks[l], t_p, t_s)

    c_p, w_gates_bf = _conv_prompt(
        cu, w_dw[l], b_dw[l], g_conv_ln[l], b_conv_ln[l], t_p,
        lambda n_steps: _slab_stream(w_in_l, (D_MODEL // 2, bw), 1, n_steps, col0=ga_blk0 * bw))
    cu_s_t = cu[t_p:].reshape(b_s, t_s, CONV_DIM).transpose(1, 0, 2)
    c_s_t, conv_state_t = _conv_sample(cu_s_t, state_conv[l].transpose(1, 0, 2),
                                       w_dw[l], b_dw[l], g_conv_ln[l], b_conv_ln[l])
    c_s = c_s_t.transpose(1, 0, 2).reshape(b_s * t_s, CONV_DIM)

    btm = BIG_TOKEN_TILE
    gate_tn = 2 * bw
    n_gate_steps = 2 * D_MODEL // gate_tn
    gate_grid = (n_gate_steps, m // btm)
    gates, w_attn_o_bf, w_conv_o_bf, w_out_bf = _wres_matmul(
        [(h, w_gates_bf, gate_tn, [lambda j: j], None)],
        [],
        [(sds((m, 2 * D_MODEL), BF16), (btm, gate_tn), lambda j, i: (i, j))],
        _epi_sigmoid, n_col_steps=n_gate_steps, name="proj_gates", tm=btm,
        side_casts=[_slab_stream(w_attn_o[l], (HEAD_DIM, D_MODEL), *gate_grid, row_of_slab=_interleaved_head),
                    _slab_stream(w_conv_o[l], (CONV_DIM // 16, D_MODEL // 4), *gate_grid),
                    _slab_stream(w_out[l], (D_MODEL // 16, D_MODEL // 4), *gate_grid)])

    n_gate_blk = D_MODEL // (2 * bw)
    one_block = [lambda j: j]
    ffn_slab = (D_MODEL // 32, D_FF // 2)
    mix, w_fg_bf = _wres_matmul(
        [(Stacked(o_p, o_s), w_attn_o_bf, 2 * bw, one_block, None),
         (Stacked(c_p, c_s), w_conv_o_bf, 2 * bw, one_block, None)],
        [(gates, (tm, 2 * bw), lambda j, i: (i, j)),
         (gates, (tm, 2 * bw), lambda j, i: (i, j + n_gate_blk))],
        [(sds((m, D_MODEL), BF16), (tm, 2 * bw), lambda j, i: (i, j))],
        _epi_merge, n_col_steps=n_gate_blk, name="merge",
        side_casts=[_slab_stream(w_ffn_in[l], ffn_slab, n_gate_blk, m // tm, col0=0, n_cols=D_FF)])

    x1, w_fu_bf = _wres_matmul(
        [(mix, w_out_bf, 2 * bw, one_block, None)],
        [(x, (tm, 2 * bw), lambda j, i: (i, j))],
        [(sds((m, D_MODEL), F32), (tm, 2 * bw), lambda j, i: (i, j))],
        _epi_residual, n_col_steps=D_MODEL // (2 * bw), name="out_proj",
        side_casts=[_slab_stream(w_ffn_in[l], ffn_slab, D_MODEL // (2 * bw), m // tm, col0=D_FF, n_cols=D_FF)])

    h2 = _rmsnorm(x1, g_ffn_norm[l], BF16, "rmsnorm_ffn")
    ftm = FFN_TOKEN_TILE
    n_ff_blk = D_FF // FFN_BLOCK
    n_ff_steps = n_ff_blk * (m // ftm)
    act, w_ffn_out_bf = _wres_matmul(
        [(h2, (w_fg_bf, w_fu_bf), FFN_BLOCK, [lambda j: j, lambda j: j], None)],
        [],
        [(sds((m, D_FF), BF16), (ftm, FFN_BLOCK), lambda j, i: (i, j))],
        _epi_swiglu, n_col_steps=n_ff_blk, name="ffn_in", tm=ftm,
        side_casts=[_slab_stream(w_ffn_out[l], (D_FF // n_ff_steps, D_MODEL), n_ff_blk, m // ftm)])

    x2 = _ares_matmul_residual(act, w_ffn_out_bf, x1, tm=tm, tn=512, name="ffn_out")
    norm_tile = NORM_TILE
    y_p = _rmsnorm(x2, g_final, F32, "rmsnorm_final_prompt", row_tile0=0, n_tiles=t_p // norm_tile,
                   tm=norm_tile)
    y_s = _rmsnorm(x2, g_final, F32, "rmsnorm_final_sample", row_tile0=t_p // norm_tile,
                   n_tiles=(m - t_p) // norm_tile, tm=norm_tile)

    y_prompt = y_p.reshape(x_prompt.shape)
    y_sample = y_s.reshape(x_sample.shape)
    k_prompt = k_all[t_p - WINDOW:t_p].reshape(1, 1, WINDOW, N_KV_HEADS, HEAD_DIM)
    v_prompt = v_all[t_p - WINDOW:t_p].reshape(1, 1, WINDOW, N_KV_HEADS, HEAD_DIM)
    conv_prompt = cu[t_p - (CONV_WIDTH - 1):t_p].reshape(1, 1, CONV_WIDTH - 1, CONV_DIM)
    k_sample = k_win.reshape(1, b_s, WINDOW, N_KV_HEADS, HEAD_DIM)
    v_sample = v_win.reshape(1, b_s, WINDOW, N_KV_HEADS, HEAD_DIM)
    conv_sample = conv_state_t.transpose(1, 0, 2).reshape(1, b_s, CONV_WIDTH - 1, CONV_DIM)
    return (y_prompt, y_sample, k_prompt, v_prompt, conv_prompt, k_sample, v_sample, conv_sample)
```
